```python
import math
import jax, jax.numpy as jnp
from jax import lax
import numpy as np

D_MODEL = 2048
BATCH = 2
SEQ = 16384
DEPTH = 2

GLA_HEADS = 4
GLA_DK = 128
GLA_DV = 256
GLA_KEY_WIDTH = GLA_HEADS * GLA_DK
GLA_VAL_WIDTH = GLA_HEADS * GLA_DV
GLA_GATE_RANK = 16
GLA_GATE_TAU = 16.0
GLA_CHUNK = 64

POOL_WINDOWS = (2, 4, 8, 16)
POOL_GROUPS = 4
POOL_WIDTH = D_MODEL // 2
POOL_GROUP_DIM = POOL_WIDTH // POOL_GROUPS

PROJ_SECTIONS = (GLA_KEY_WIDTH, GLA_KEY_WIDTH, GLA_VAL_WIDTH, GLA_VAL_WIDTH,
                 GLA_GATE_RANK, GLA_GATE_RANK, POOL_WIDTH, D_MODEL, D_MODEL)
D_IN_PROJ = 512 + 512 + 1024 + 1024 + 16 + 16 + 1024 + 2048 + 2048

PEER_HEADS = 8
PEER_NKEYS = 128
PEER_EXPERTS = PEER_NKEYS * PEER_NKEYS
PEER_QDIM = 256
PEER_HALF = PEER_QDIM // 2
PEER_TOPK = 16
PEER_BLOCK = 128

RMS_EPS = 1e-6

kernel_name = "hybrid_gla_pool_peer_encoder"


def rms_norm(x, g):
    xf = x.astype(jnp.float32)
    var = jnp.mean(xf * xf, axis=-1, keepdims=True)
    return (xf * lax.rsqrt(var + RMS_EPS) * g.astype(jnp.float32)).astype(x.dtype)


def split_proj(proj):
    points = []
    acc = 0
    for size in PROJ_SECTIONS[:-1]:
        acc += size
        points.append(acc)
    return jnp.split(proj, points, axis=-1)


def gla_chunked(q, k, v, logg, strict):
    B, S, H, DK = q.shape
    DV = v.shape[-1]
    nc = S // GLA_CHUNK
    q = q.reshape(B, nc, GLA_CHUNK, H, DK)
    k = k.reshape(B, nc, GLA_CHUNK, H, DK)
    v = v.reshape(B, nc, GLA_CHUNK, H, DV)
    b = jnp.cumsum(logg.reshape(B, nc, GLA_CHUNK, H, DK), axis=2)
    qe = q * jnp.exp(b)
    ke = k * jnp.exp(-b)
    attn = jnp.einsum('bnihk,bnjhk->bnhij', qe, ke)
    offset = -1 if strict else 0
    mask = jnp.tril(jnp.ones((GLA_CHUNK, GLA_CHUNK), dtype=bool), k=offset)
    attn = jnp.where(mask, attn, 0.0)
    o_intra = jnp.einsum('bnhij,bnjhv->bnihv', attn, v)
    b_last = b[:, :, -1]
    kd = k * jnp.exp(b_last[:, :, None] - b)
    chunk_kv = jnp.einsum('bnjhk,bnjhv->bnhkv', kd, v)
    decay = jnp.exp(b_last)

    def step(state, xs):
        d, kv = xs
        return d[..., None] * state + kv, state

    init = jnp.zeros((B, H, DK, DV), jnp.float32)
    _, s_prev = lax.scan(step, init, (jnp.moveaxis(decay, 1, 0), jnp.moveaxis(chunk_kv, 1, 0)))
    o_inter = jnp.einsum('bnihk,nbhkv->bnihv', qe, s_prev)
    return (o_intra + o_inter).reshape(B, S, H, DV)


def gla_branch(q, k, v, r, a_f, a_b, wg_f, bg_f, wg_b, bg_b, norm_g):
    B, S, _ = q.shape
    f32 = jnp.float32
    qh = q.astype(f32).reshape(B, S, GLA_HEADS, GLA_DK) * (GLA_DK ** -0.5)
    kh = k.astype(f32).reshape(B, S, GLA_HEADS, GLA_DK)
    vh = v.astype(f32).reshape(B, S, GLA_HEADS, GLA_DV)
    logg_f = (jax.nn.log_sigmoid(a_f.astype(f32) @ wg_f.astype(f32) + bg_f.astype(f32)) / GLA_GATE_TAU
              ).reshape(B, S, GLA_HEADS, GLA_DK)
    logg_b = (jax.nn.log_sigmoid(a_b.astype(f32) @ wg_b.astype(f32) + bg_b.astype(f32)) / GLA_GATE_TAU
              ).reshape(B, S, GLA_HEADS, GLA_DK)
    o_f = gla_chunked(qh, kh, vh, logg_f, strict=False)
    flip = lambda t: jnp.flip(t, axis=1)
    o_b = flip(gla_chunked(flip(qh), flip(kh), flip(vh), flip(logg_b), strict=True))
    o = o_f + o_b
    var = jnp.mean(o * o, axis=-1, keepdims=True)
    o = o * lax.rsqrt(var + RMS_EPS) * norm_g.astype(f32).reshape(GLA_HEADS, GLA_DV)
    o = o.reshape(B, S, GLA_VAL_WIDTH) * jax.nn.silu(r.astype(f32))
    return o.astype(q.dtype)


def multiscale_pool(p, w_grp, scale):
    B, S, _ = p.shape
    f32 = jnp.float32
    pg = p.astype(f32).reshape(B, S, POOL_GROUPS, POOL_GROUP_DIM)
    t = jnp.arange(S)
    outs = []
    for gi, w in enumerate(POOL_WINDOWS):
        xg = pg[:, :, gi]
        cs = jnp.concatenate([jnp.zeros((B, 1, POOL_GROUP_DIM), f32), jnp.cumsum(xg, axis=1)], axis=1)
        lo = jnp.clip(t - w // 2, 0, S)
        hi = jnp.clip(t - w // 2 + w, 0, S)
        win_sum = jnp.take(cs, hi, axis=1) - jnp.take(cs, lo, axis=1)
        cnt = (hi - lo).astype(f32)[None, :, None]
        outs.append(win_sum / cnt - xg)
    pooled = jnp.stack(outs, axis=2)
    mixed = jnp.einsum('bsgc,gcd->bsgd', pooled, w_grp.astype(f32))
    return (mixed.reshape(B, S, POOL_WIDTH) * scale.astype(f32)).astype(p.dtype)


def peer_ffn(x, wq, subkeys, u, v):
    B, S, D = x.shape
    K = PEER_TOPK
    q = (x @ wq).astype(jnp.float32).reshape(B, S, PEER_HEADS, 2, PEER_HALF)
    scores = jnp.einsum('bshpc,hpnc->bshpn', q, subkeys.astype(jnp.float32))
    s_top, i_top = lax.top_k(scores, K)
    cand = (s_top[..., 0, :, None] + s_top[..., 1, None, :]).reshape(B, S, PEER_HEADS, K * K)
    best, flat = lax.top_k(cand, K)
    i1 = jnp.take_along_axis(i_top[..., 0, :], flat // K, axis=-1)
    i2 = jnp.take_along_axis(i_top[..., 1, :], flat % K, axis=-1)
    expert = i1 * PEER_NKEYS + i2
    gate = jax.nn.softmax(best, axis=-1).astype(x.dtype)
    nblk = (B * S) // PEER_BLOCK
    xb = x.reshape(nblk, PEER_BLOCK, D)
    eb = expert.reshape(nblk, PEER_BLOCK, PEER_HEADS, K)
    gb = gate.reshape(nblk, PEER_BLOCK, PEER_HEADS, K)

    def block(args):
        xt, et, gt = args
        ue = jnp.take(u, et, axis=0)
        act = jax.nn.gelu(jnp.einsum('thkd,td->thk', ue, xt), approximate=False)
        ve = jnp.take(v, et, axis=0)
        return jnp.einsum('thk,thkd->td', act * gt, ve)

    out = lax.map(block, (xb, eb, gb))
    return out.reshape(B, S, D)


def setup_inputs(seed: int = 0) -> dict:
    key = jax.random.key(seed)
    ks = jax.random.split(key, 20)
    n = lambda k, shape, s: jax.random.normal(k, shape, jnp.float32) * s
    L, D = DEPTH, D_MODEL
    return {
        "x": n(ks[0], (BATCH, SEQ, D), 1.0),
        "norm_mix_g": 1.0 + n(ks[1], (L, D), 0.02),
        "w_in": n(ks[2], (L, D, D_IN_PROJ), D ** -0.5),
        "gla_gate_w_fwd": n(ks[3], (L, GLA_GATE_RANK, GLA_KEY_WIDTH), GLA_GATE_RANK ** -0.5),
        "gla_gate_b_fwd": n(ks[4], (L, GLA_KEY_WIDTH), 0.1),
        "gla_gate_w_bwd": n(ks[5], (L, GLA_GATE_RANK, GLA_KEY_WIDTH), GLA_GATE_RANK ** -0.5),
        "gla_gate_b_bwd": n(ks[6], (L, GLA_KEY_WIDTH), 0.1),
        "gla_norm_g": 1.0 + n(ks[7], (L, GLA_VAL_WIDTH), 0.02),
        "w_up_gla": n(ks[8], (L, GLA_VAL_WIDTH, D), GLA_VAL_WIDTH ** -0.5),
        "pool_w": n(ks[9], (L, POOL_GROUPS, POOL_GROUP_DIM, POOL_GROUP_DIM), POOL_GROUP_DIM ** -0.5),
        "pool_scale": 1.0 + n(ks[10], (L, POOL_WIDTH), 0.02),
        "w_up_pool": n(ks[11], (L, POOL_WIDTH, D), POOL_WIDTH ** -0.5),
        "w_out": n(ks[12], (L, D, D), D ** -0.5),
        "norm_ffn_g": 1.0 + n(ks[13], (L, D), 0.02),
        "peer_wq": n(ks[14], (L, D, PEER_HEADS * PEER_QDIM), D ** -0.5),
        "peer_subkeys": n(ks[15], (L, PEER_HEADS, 2, PEER_NKEYS, PEER_HALF), PEER_HALF ** -0.5),
        "peer_u": n(ks[16], (L, PEER_EXPERTS, D), D ** -0.5),
        "peer_v": n(ks[17], (L, PEER_EXPERTS, D), PEER_HEADS ** -0.5),
        "norm_final_g": 1.0 + n(ks[18], (D,), 0.02),
    }


def reference(x, norm_mix_g, w_in, gla_gate_w_fwd, gla_gate_b_fwd, gla_gate_w_bwd, gla_gate_b_bwd,
              gla_norm_g, w_up_gla, pool_w, pool_scale, w_up_pool, w_out, norm_ffn_g,
              peer_wq, peer_subkeys, peer_u, peer_v, norm_final_g):
    for l in range(DEPTH):
        h = rms_norm(x, norm_mix_g[l])
        proj = h @ w_in[l]
        q, k, v, r, a_f, a_b, p, g_gla, g_pool = split_proj(proj)
        y_gla = gla_branch(q, k, v, r, a_f, a_b, gla_gate_w_fwd[l], gla_gate_b_fwd[l],
                           gla_gate_w_bwd[l], gla_gate_b_bwd[l], gla_norm_g[l]) @ w_up_gla[l]
        y_pool = multiscale_pool(p, pool_w[l], pool_scale[l]) @ w_up_pool[l]
        merged = jax.nn.sigmoid(g_gla) * y_gla + jax.nn.sigmoid(g_pool) * y_pool
        x = x + merged @ w_out[l]
        h = rms_norm(x, norm_ffn_g[l])
        x = x + peer_ffn(h, peer_wq[l], peer_subkeys[l], peer_u[l], peer_v[l])
    return rms_norm(x, norm_final_g)
```

```python
import functools

import numpy as np
import jax
import jax.numpy as jnp
from jax import lax
from jax.experimental import pallas as pl
from jax.experimental.pallas import tpu as pltpu

F32 = jnp.float32
BF16 = jnp.bfloat16

D_MODEL = 2048
DEPTH = 2
RMS_EPS = 1e-6

GLA_HEADS = 4
GLA_DK = 128
GLA_DV = 256
GLA_KEY_WIDTH = GLA_HEADS * GLA_DK
GLA_VAL_WIDTH = GLA_HEADS * GLA_DV
GLA_GATE_RANK = 16
GLA_GATE_TAU = 16.0
GLA_CHUNK = 64

POOL_WINDOWS = (2, 4, 8, 16)
POOL_WIDTH = D_MODEL // 2
POOL_GROUP_DIM = POOL_WIDTH // len(POOL_WINDOWS)
POOL_HALO = 16

PEER_HEADS = 8
PEER_NKEYS = 128
PEER_EXPERTS = PEER_NKEYS * PEER_NKEYS
PEER_HALF = 128
PEER_TOPK = 16
PEER_SLOTS = PEER_HEADS * PEER_TOPK

COL_Q = 0
COL_K = COL_Q + GLA_KEY_WIDTH
COL_V = COL_K + GLA_KEY_WIDTH
COL_R = COL_V + GLA_VAL_WIDTH
COL_P = COL_R + GLA_VAL_WIDTH
COL_GG = COL_P + POOL_WIDTH
COL_GP = COL_GG + D_MODEL
PROJ_MAIN = COL_GP + D_MODEL
GATE_PAD = 128

VMEM_LIMIT_BYTES = 56 * 1024 * 1024

NT_DIMS = (((1,), (1,)), ((), ()))
TN_DIMS = (((0,), (0,)), ((), ()))


def _tiles(T, S):
    return dict(
        proj_tm=min(1024, T), proj_tn=1024,
        gla_tb=min(512, S),
        mix_tm=min(256, S),
        sel_tq=min(128, T),
        gate_tg=min(128, T),
        dense_tm=min(512, T), dense_te=512,
        norm_tm=min(1024, T),
    )


def _params(*sem):
    return pltpu.CompilerParams(dimension_semantics=sem, vmem_limit_bytes=VMEM_LIMIT_BYTES)


def _rms(x, g):
    var = jnp.mean(x * x, axis=-1, keepdims=True)
    return x * lax.rsqrt(var + RMS_EPS) * g


def _norm_matmul_body(x_ref, g_ref, w_ref, *out_refs, emit_h):
    if emit_h:
        o_ref, h_out_ref, h_ref = out_refs
    else:
        o_ref, h_ref = out_refs

    @pl.when(pl.program_id(1) == 0)
    def _():
        h = _rms(x_ref[...], g_ref[...]).astype(BF16)
        h_ref[...] = h
        if emit_h:
            h_out_ref[...] = h

    o_ref[...] = jnp.dot(h_ref[...], w_ref[...], preferred_element_type=F32).astype(o_ref.dtype)


def _norm_matmul(x, g, w, out_dtype, tm, tn, emit_h=False):
    T, D = x.shape
    N = w.shape[1]
    tn = min(tn, N)
    out_shape = [jax.ShapeDtypeStruct((T, N), out_dtype)]
    out_specs = [pl.BlockSpec((tm, tn), lambda i, j: (i, j))]
    if emit_h:
        out_shape.append(jax.ShapeDtypeStruct((T, D), BF16))
        out_specs.append(pl.BlockSpec((tm, D), lambda i, j: (i, 0)))
    res = pl.pallas_call(
        functools.partial(_norm_matmul_body, emit_h=emit_h),
        grid=(T // tm, N // tn),
        in_specs=[
            pl.BlockSpec((tm, D), lambda i, j: (i, 0)),
            pl.BlockSpec((1, D), lambda i, j: (0, 0)),
            pl.BlockSpec((D, tn), lambda i, j: (0, j)),
        ],
        out_specs=out_specs,
        out_shape=out_shape,
        scratch_shapes=[pltpu.VMEM((tm, D), BF16)],
        compiler_params=_params("parallel", "arbitrary"),
        name="norm_matmul",
    )(x, g.reshape(1, D), w)
    return res if emit_h else res[0]


def _gla_body(*refs, reverse, final, nch):
    if final:
        (q_ref, k_ref, v_ref, a_ref, wg_ref, bg_ref, oprev_ref, r_ref, ng_ref,
         o_ref, st_ref) = refs
    else:
        q_ref, k_ref, v_ref, a_ref, wg_ref, bg_ref, o_ref, st_ref = refs
    C = GLA_CHUNK

    @pl.when(pl.program_id(2) == 0)
    def _():
        st_ref[...] = jnp.zeros_like(st_ref)

    z = jnp.dot(a_ref[...], wg_ref[...], precision=lax.Precision.HIGHEST,
                preferred_element_type=F32) + bg_ref[...]
    logg = (jnp.minimum(z, 0.0) - jnp.log(1.0 + jnp.exp(-jnp.abs(z)))) * (1.0 / GLA_GATE_TAU)

    ri = lax.broadcasted_iota(jnp.int32, (C, C), 0)
    ci = lax.broadcasted_iota(jnp.int32, (C, C), 1)
    if reverse:
        tri = jnp.where(ci >= ri, 1.0, 0.0).astype(BF16)
        amask = ci > ri
    else:
        tri = jnp.where(ci <= ri, 1.0, 0.0).astype(BF16)
        amask = ci <= ri
    scale = GLA_DK ** -0.5

    order = range(nch - 1, -1, -1) if reverse else range(nch)
    for c in order:
        sl = slice(c * C, (c + 1) * C)
        lg = logg[sl, :]
        lg_hi = lg.astype(BF16)
        lg_lo = (lg - lg_hi.astype(F32)).astype(BF16)
        bcum = (jnp.dot(tri, lg_hi, preferred_element_type=F32)
                + jnp.dot(tri, lg_lo, preferred_element_type=F32))
        btot = bcum[0:1, :] if reverse else bcum[C - 1:C, :]
        q = q_ref[sl, :].astype(F32) * scale
        k = k_ref[sl, :].astype(F32)
        v = v_ref[sl, :]
        qe = (q * jnp.exp(bcum)).astype(BF16)
        ke = (k * jnp.exp(-bcum)).astype(BF16)
        kd = (k * jnp.exp(btot - bcum)).astype(BF16)
        attn = lax.dot_general(qe, ke, NT_DIMS, preferred_element_type=F32)
        attn = jnp.where(amask, attn, 0.0).astype(BF16)
        st = st_ref[...]
        o = (jnp.dot(attn, v, preferred_element_type=F32)
             + lax.dot_general(qe, st.astype(BF16), NT_DIMS, preferred_element_type=F32))
        st_ref[...] = jnp.exp(btot) * st + lax.dot_general(
            v, kd, TN_DIMS, preferred_element_type=F32)
        if final:
            o = o + oprev_ref[sl, :]
            var = jnp.mean(o * o, axis=-1, keepdims=True)
            o = o * lax.rsqrt(var + RMS_EPS) * ng_ref[...]
            r = r_ref[sl, :].astype(F32)
            o = o * (r * jax.nn.sigmoid(r))
        o_ref[sl, :] = o.astype(o_ref.dtype)


def _gla_pass(proj, a, wg_pad, bg, S, tb, reverse, oprev=None, norm_g=None):
    T = proj.shape[0]
    B = T // S
    nb = S // tb
    final = oprev is not None

    def row(b, n):
        return b * nb + ((nb - 1 - n) if reverse else n)

    kq, kk = COL_Q // GLA_DK, COL_K // GLA_DK
    kv, kr = COL_V // GLA_DV, COL_R // GLA_DV
    in_specs = [
        pl.BlockSpec((tb, GLA_DK), lambda b, h, n: (row(b, n), kq + h)),
        pl.BlockSpec((tb, GLA_DK), lambda b, h, n: (row(b, n), kk + h)),
        pl.BlockSpec((tb, GLA_DV), lambda b, h, n: (row(b, n), kv + h)),
        pl.BlockSpec((tb, GATE_PAD), lambda b, h, n: (row(b, n), 0)),
        pl.BlockSpec((GATE_PAD, GLA_DK), lambda b, h, n: (0, h)),
        pl.BlockSpec((1, GLA_DK), lambda b, h, n: (0, h)),
    ]
    args = [proj, proj, proj, a, wg_pad, bg.reshape(1, GLA_KEY_WIDTH)]
    if final:
        in_specs += [
            pl.BlockSpec((tb, GLA_DV), lambda b, h, n: (row(b, n), h)),
            pl.BlockSpec((tb, GLA_DV), lambda b, h, n: (row(b, n), kr + h)),
            pl.BlockSpec((1, GLA_DV), lambda b, h, n: (0, h)),
        ]
        args += [oprev, proj, norm_g.reshape(1, GLA_VAL_WIDTH)]
    return pl.pallas_call(
        functools.partial(_gla_body, reverse=reverse, final=final, nch=tb // GLA_CHUNK),
        grid=(B, GLA_HEADS, nb),
        in_specs=in_specs,
        out_specs=pl.BlockSpec((tb, GLA_DV), lambda b, h, n: (row(b, n), h)),
        out_shape=jax.ShapeDtypeStruct((T, GLA_VAL_WIDTH), BF16 if final else F32),
        scratch_shapes=[pltpu.VMEM((GLA_DV, GLA_DK), F32)],
        compiler_params=_params("parallel", "parallel", "arbitrary"),
        name="gla_fwd" if final else "gla_bwd",
    )(*args)


def _mix_body(y_ref, pm_ref, pp_ref, pn_ref, gg_ref, gp_ref, x_ref, wug_ref, pw_ref,
              ps_ref, wup_ref, wo_ref, o_ref, *, tm, S):
    H = POOL_HALO
    G = POOL_GROUP_DIM
    r = lax.rem(pl.program_id(0), S // tm)
    pm = pm_ref[...].astype(F32)
    pp = jnp.where(r == 0, 0.0, pp_ref[...].astype(F32))
    pn = jnp.where(r == S // tm - 1, 0.0, pn_ref[...].astype(F32))
    ext = jnp.concatenate([pp, pm, pn], axis=0)
    n = tm + 2 * H
    pos = r * tm + lax.broadcasted_iota(jnp.int32, (tm, G), 0)
    feats = []
    for gi, w in enumerate(POOL_WINDOWS):
        cols = slice(gi * G, (gi + 1) * G)
        s = ext[:, cols]
        span = 1
        while span < w:
            s = s + pltpu.roll(s, n - span, axis=0)
            span *= 2
        centred = pltpu.roll(s, w // 2, axis=0)[H:H + tm, :]
        lo = jnp.clip(pos - w // 2, 0, S)
        hi = jnp.clip(pos - w // 2 + w, 0, S)
        pooled = centred / (hi - lo).astype(F32) - pm[:, cols]
        feats.append(jnp.dot(pooled.astype(BF16), pw_ref[gi], preferred_element_type=F32))
    feat = (jnp.concatenate(feats, axis=1) * ps_ref[...]).astype(BF16)
    y_pool = jnp.dot(feat, wup_ref[...], preferred_element_type=F32)
    y_gla = jnp.dot(y_ref[...], wug_ref[...], preferred_element_type=F32)
    merged = (jax.nn.sigmoid(gg_ref[...].astype(F32)) * y_gla
              + jax.nn.sigmoid(gp_ref[...].astype(F32)) * y_pool)
    o_ref[...] = x_ref[...] + jnp.dot(merged.astype(BF16), wo_ref[...],
                                      preferred_element_type=F32)


def _mix(y, proj, x, wug, pw, ps, wup, wo, S, tm):
    T, D = x.shape
    H = POOL_HALO
    last_halo = T // H - 1
    kp = COL_P // POOL_WIDTH
    const2 = lambda i: (0, 0)
    return pl.pallas_call(
        functools.partial(_mix_body, tm=tm, S=S),
        grid=(T // tm,),
        in_specs=[
            pl.BlockSpec((tm, GLA_VAL_WIDTH), lambda i: (i, 0)),
            pl.BlockSpec((tm, POOL_WIDTH), lambda i: (i, kp)),
            pl.BlockSpec((H, POOL_WIDTH), lambda i: (jnp.maximum(i * (tm // H) - 1, 0), kp)),
            pl.BlockSpec((H, POOL_WIDTH),
                         lambda i: (jnp.minimum((i + 1) * (tm // H), last_halo), kp)),
            pl.BlockSpec((tm, D), lambda i: (i, COL_GG // D_MODEL)),
            pl.BlockSpec((tm, D), lambda i: (i, COL_GP // D_MODEL)),
            pl.BlockSpec((tm, D), lambda i: (i, 0)),
            pl.BlockSpec(wug.shape, const2),
            pl.BlockSpec(pw.shape, lambda i: (0, 0, 0)),
            pl.BlockSpec((1, POOL_WIDTH), const2),
            pl.BlockSpec(wup.shape, const2),
            pl.BlockSpec(wo.shape, const2),
        ],
        out_specs=pl.BlockSpec((tm, D), lambda i: (i, 0)),
        out_shape=jax.ShapeDtypeStruct((T, D), F32),
        compiler_params=_params("parallel"),
        name="mix_out",
    )(y, proj, proj, proj, proj, proj, x, wug, pw, ps.reshape(1, POOL_WIDTH), wup, wo)


def _split_bf16(x):
    hi = x.astype(BF16)
    return hi, (x - hi.astype(F32)).astype(BF16)


def _peer_select_body(q_ref, sub_ref, sel_ref, gate_ref, eid_ref, *, tq):
    K = PEER_TOPK
    NEG = -jnp.inf
    key_iota = lax.broadcasted_iota(jnp.int32, (PEER_NKEYS, tq), 0)
    k_iota = lax.broadcasted_iota(jnp.int32, (K, tq), 0)

    def scores(h, p):
        qg = q_ref[:, pl.ds(pl.multiple_of(h * 2 * PEER_HALF + p * PEER_HALF, PEER_HALF),
                            PEER_HALF)]
        q_hi, q_lo = _split_bf16(qg)
        s_hi, s_lo = _split_bf16(sub_ref[h, p])
        dot = lambda a, b: lax.dot_general(a, b, NT_DIMS, preferred_element_type=F32)
        return dot(s_hi, q_hi) + (dot(s_hi, q_lo) + dot(s_lo, q_hi))

    def top_keys(s):
        rows, vals, idxs = [], jnp.zeros((K, tq), F32), jnp.zeros((K, tq), jnp.int32)
        idx_rows = []
        for r in range(K):
            m = jnp.max(s, axis=0, keepdims=True)
            pos = jnp.min(jnp.where(s == m, key_iota, PEER_NKEYS), axis=0, keepdims=True)
            s = jnp.where(key_iota == pos, NEG, s)
            rows.append(m)
            idx_rows.append(pos)
            vals = jnp.where(k_iota == r, m, vals)
            idxs = jnp.where(k_iota == r, pos, idxs)
        return rows, idx_rows, vals, idxs

    n_b = [K // (a + 1) for a in range(K)]
    n_rows = K + 8 * 7 + 8
    cand_iota = lax.broadcasted_iota(jnp.int32, (n_rows, tq), 0)
    b8 = lax.broadcasted_iota(jnp.int32, (8, tq), 0)

    def head(h, carry):
        r1, ir1, v1, i1 = top_keys(scores(h, 0))
        r2, ir2, v2, i2 = top_keys(scores(h, 1))
        cand = [r1[0] + v2]
        eid = [ir1[0] * PEER_NKEYS + i2]
        for a in range(1, 8):
            cand.append(jnp.where(b8 < n_b[a], r1[a] + v2[0:8, :], NEG))
            eid.append(ir1[a] * PEER_NKEYS + i2[0:8, :])
        cand.append(v1[8:16, :] + r2[0])
        eid.append(i1[8:16, :] * PEER_NKEYS + ir2[0])
        cand = jnp.concatenate(cand, axis=0)
        eid = jnp.concatenate(eid, axis=0)
        best = jnp.zeros((K, tq), F32)
        chosen = jnp.zeros((K, tq), jnp.int32)
        for r in range(K):
            m = jnp.max(cand, axis=0, keepdims=True)
            pos = jnp.min(jnp.where(cand == m, cand_iota, n_rows), axis=0, keepdims=True)
            hit = cand_iota == pos
            e = jnp.max(jnp.where(hit, eid, -1), axis=0, keepdims=True)
            cand = jnp.where(hit, NEG, cand)
            best = jnp.where(k_iota == r, m, best)
            chosen = jnp.where(k_iota == r, e, chosen)
        ex = jnp.exp(best - jnp.max(best, axis=0, keepdims=True))
        gate = ex / jnp.sum(ex, axis=0, keepdims=True)
        rows = pl.ds(pl.multiple_of(h * K, K), K)
        gate_ref[rows, :] = gate
        eid_ref[rows, :] = chosen
        return carry

    lax.fori_loop(0, PEER_HEADS, head, 0)
    e = eid_ref[...]
    n = PEER_SLOTS
    sel_ref[:, 0:n] = jnp.right_shift(e, 7).astype(F32).T
    sel_ref[:, n:2 * n] = jnp.bitwise_and(e, PEER_NKEYS - 1).astype(F32).T
    sel_ref[:, 2 * n:3 * n] = gate_ref[...].T


def _peer_select(q, sub, tq):
    T, QD = q.shape
    return pl.pallas_call(
        functools.partial(_peer_select_body, tq=tq),
        grid=(T // tq,),
        in_specs=[
            pl.BlockSpec((tq, QD), lambda i: (i, 0)),
            pl.BlockSpec(sub.shape, lambda i: (0, 0, 0, 0)),
        ],
        out_specs=pl.BlockSpec((tq, 3 * PEER_SLOTS), lambda i: (i, 0)),
        out_shape=jax.ShapeDtypeStruct((T, 3 * PEER_SLOTS), F32),
        scratch_shapes=[pltpu.VMEM((PEER_SLOTS, tq), F32),
                        pltpu.VMEM((PEER_SLOTS, tq), jnp.int32)],
        compiler_params=_params("parallel"),
        name="peer_select",
    )(q, sub)


def _peer_gates_body(sel_ref, g_ref, *, tg):
    n = PEER_SLOTS
    row_id = lax.broadcasted_iota(jnp.int32, (PEER_NKEYS, n), 0).astype(F32)

    def token(t, carry):
        row = sel_ref[pl.ds(t, 1), :]
        i1, i2, gate = row[:, 0:n], row[:, n:2 * n], row[:, 2 * n:3 * n]
        p1 = jnp.where(row_id == i1, gate, 0.0).astype(BF16)
        p2 = jnp.where(row_id == i2, 1.0, 0.0).astype(BF16)
        g_ref[t] = lax.dot_general(p1, p2, NT_DIMS, preferred_element_type=F32).astype(BF16)
        return carry

    lax.fori_loop(0, tg, token, 0)


def _peer_gates(sel, tg):
    T = sel.shape[0]
    return pl.pallas_call(
        functools.partial(_peer_gates_body, tg=tg),
        grid=(T // tg,),
        in_specs=[pl.BlockSpec((tg, 3 * PEER_SLOTS), lambda i: (i, 0))],
        out_specs=pl.BlockSpec((tg, PEER_NKEYS, PEER_NKEYS), lambda i: (i, 0, 0)),
        out_shape=jax.ShapeDtypeStruct((T, PEER_NKEYS, PEER_NKEYS), BF16),
        compiler_params=_params("parallel"),
        name="peer_gates",
    )(sel)


SQRT_HALF = np.float32(np.sqrt(0.5))


def _peer_dense_body(h_ref, g_ref, u_ref, v_ref, x_ref, o_ref):
    @pl.when(pl.program_id(1) == 0)
    def _():
        o_ref[...] = x_ref[...]

    act = lax.dot_general(h_ref[...], u_ref[...], NT_DIMS, preferred_element_type=F32)
    gelu = 0.5 * act * (1.0 + lax.erf(act * SQRT_HALF))
    w = (gelu * g_ref[...].astype(F32)).astype(BF16)
    o_ref[...] += jnp.dot(w, v_ref[...], preferred_element_type=F32)


def _peer_dense(h, gates, u, v, x, tm, te):
    T, D = x.shape
    E = u.shape[0]
    return pl.pallas_call(
        _peer_dense_body,
        grid=(T // tm, E // te),
        in_specs=[
            pl.BlockSpec((tm, D), lambda i, j: (i, 0)),
            pl.BlockSpec((tm, te), lambda i, j: (i, j)),
            pl.BlockSpec((te, D), lambda i, j: (j, 0)),
            pl.BlockSpec((te, D), lambda i, j: (j, 0)),
            pl.BlockSpec((tm, D), lambda i, j: (i, 0)),
        ],
        out_specs=pl.BlockSpec((tm, D), lambda i, j: (i, 0)),
        out_shape=jax.ShapeDtypeStruct((T, D), F32),
        compiler_params=_params("parallel", "arbitrary"),
        name="peer_dense",
    )(h, gates, u, v, x)


def _final_norm_body(x_ref, g_ref, o_ref):
    o_ref[...] = _rms(x_ref[...], g_ref[...])


def _final_norm(x, g, tm):
    T, D = x.shape
    return pl.pallas_call(
        _final_norm_body,
        grid=(T // tm,),
        in_specs=[pl.BlockSpec((tm, D), lambda i: (i, 0)), pl.BlockSpec((1, D), lambda i: (0, 0))],
        out_specs=pl.BlockSpec((tm, D), lambda i: (i, 0)),
        out_shape=jax.ShapeDtypeStruct((T, D), F32),
        compiler_params=_params("parallel"),
        name="final_norm",
    )(x, g.reshape(1, D))


def _split_w_in(w):
    kw, vw, rk = GLA_KEY_WIDTH, GLA_VAL_WIDTH, GLA_GATE_RANK
    a0 = 2 * kw + 2 * vw
    main = jnp.concatenate([w[:, :a0], w[:, a0 + 2 * rk:]], axis=1).astype(BF16)
    gate = jnp.pad(w[:, a0:a0 + 2 * rk], ((0, 0), (0, GATE_PAD - 2 * rk))).astype(BF16)
    return main, gate


def _pad_gate_w(wg, first_row):
    return jnp.pad(wg, ((first_row, GATE_PAD - first_row - GLA_GATE_RANK), (0, 0)))


def kernel(x, norm_mix_g, w_in, gla_gate_w_fwd, gla_gate_b_fwd, gla_gate_w_bwd, gla_gate_b_bwd, gla_norm_g, w_up_gla, pool_w, pool_scale, w_up_pool, w_out, norm_ffn_g, peer_wq, peer_subkeys, peer_u, peer_v, norm_final_g):
    B, S, D = x.shape
    T = B * S
    t = _tiles(T, S)
    xf = x.reshape(T, D)
    for l in range(DEPTH):
        w_main, w_gate = _split_w_in(w_in[l])
        proj = _norm_matmul(xf, norm_mix_g[l], w_main, BF16, t["proj_tm"], t["proj_tn"])
        a = _norm_matmul(xf, norm_mix_g[l], w_gate, F32, t["proj_tm"], t["proj_tn"])
        o_bwd = _gla_pass(proj, a, _pad_gate_w(gla_gate_w_bwd[l], GLA_GATE_RANK),
                          gla_gate_b_bwd[l], S, t["gla_tb"], reverse=True)
        y_gla = _gla_pass(proj, a, _pad_gate_w(gla_gate_w_fwd[l], 0), gla_gate_b_fwd[l],
                          S, t["gla_tb"], reverse=False, oprev=o_bwd, norm_g=gla_norm_g[l])
        xf = _mix(y_gla, proj, xf, w_up_gla[l].astype(BF16), pool_w[l].astype(BF16),
                  pool_scale[l], w_up_pool[l].astype(BF16), w_out[l].astype(BF16),
                  S, t["mix_tm"])
        q, h = _norm_matmul(xf, norm_ffn_g[l], peer_wq[l].astype(BF16), F32,
                            t["proj_tm"], t["proj_tn"], emit_h=True)
        sel = _peer_select(q, peer_subkeys[l], t["sel_tq"])
        gates = _peer_gates(sel, t["gate_tg"]).reshape(T, PEER_EXPERTS)
        xf = _peer_dense(h, gates, peer_u[l].astype(BF16), peer_v[l].astype(BF16), xf,
                         t["dense_tm"], t["dense_te"])
    return _final_norm(xf, norm_final_g, t["norm_tm"]).reshape(B, S, D)
```

```python
import functools

import numpy as np
import jax
import jax.numpy as jnp
from jax import lax
from jax.experimental import pallas as pl
from jax.experimental.pallas import tpu as pltpu

F32 = jnp.float32
BF16 = jnp.bfloat16

D_MODEL = 2048
DEPTH = 2
RMS_EPS = 1e-6

GLA_HEADS = 4
GLA_DK = 128
GLA_DV = 256
GLA_KEY_WIDTH = GLA_HEADS * GLA_DK
GLA_VAL_WIDTH = GLA_HEADS * GLA_DV
GLA_GATE_RANK = 16
GLA_GATE_TAU = 16.0
GLA_CHUNK = 64

POOL_WINDOWS = (2, 4, 8, 16)
POOL_WIDTH = D_MODEL // 2
POOL_GROUP_DIM = POOL_WIDTH // len(POOL_WINDOWS)
POOL_HALO = 16

PEER_HEADS = 8
PEER_NKEYS = 128
PEER_EXPERTS = PEER_NKEYS * PEER_NKEYS
PEER_HALF = 128
PEER_TOPK = 16
PEER_SLOTS = PEER_HEADS * PEER_TOPK

COL_Q = 0
COL_K = COL_Q + GLA_KEY_WIDTH
COL_V = COL_K + GLA_KEY_WIDTH
COL_R = COL_V + GLA_VAL_WIDTH
COL_P = COL_R + GLA_VAL_WIDTH
COL_GG = COL_P + POOL_WIDTH
COL_GP = COL_GG + D_MODEL
PROJ_MAIN = COL_GP + D_MODEL
GATE_PAD = 128

VMEM_LIMIT_BYTES = 56 * 1024 * 1024

NT_DIMS = (((1,), (1,)), ((), ()))
TN_DIMS = (((0,), (0,)), ((), ()))


def _tiles(T, S):
    return dict(
        proj_tm=min(1024, T), proj_tn=1024,
        gla_tb=min(512, S),
        mix_tm=min(256, S),
        sel_tq=min(128, T),
        gate_tg=min(128, T),
        dense_tm=min(1024, T), dense_te=512,
        norm_tm=min(1024, T),
    )


def _params(*sem):
    return pltpu.CompilerParams(dimension_semantics=sem, vmem_limit_bytes=VMEM_LIMIT_BYTES)


def _rms(x, g):
    var = jnp.mean(x * x, axis=-1, keepdims=True)
    return x * lax.rsqrt(var + RMS_EPS) * g


def _norm_matmul_body(x_ref, g_ref, w_ref, *out_refs, emit_h):
    if emit_h:
        o_ref, h_out_ref, h_ref = out_refs
    else:
        o_ref, h_ref = out_refs

    @pl.when(pl.program_id(1) == 0)
    def _():
        h = _rms(x_ref[...], g_ref[...]).astype(BF16)
        h_ref[...] = h
        if emit_h:
            h_out_ref[...] = h

    o_ref[...] = jnp.dot(h_ref[...], w_ref[...], preferred_element_type=F32).astype(o_ref.dtype)


def _norm_matmul(x, g, w, out_dtype, tm, tn, emit_h=False):
    T, D = x.shape
    N = w.shape[1]
    tn = min(tn, N)
    out_shape = [jax.ShapeDtypeStruct((T, N), out_dtype)]
    out_specs = [pl.BlockSpec((tm, tn), lambda i, j: (i, j))]
    if emit_h:
        out_shape.append(jax.ShapeDtypeStruct((T, D), BF16))
        out_specs.append(pl.BlockSpec((tm, D), lambda i, j: (i, 0)))
    res = pl.pallas_call(
        functools.partial(_norm_matmul_body, emit_h=emit_h),
        grid=(T // tm, N // tn),
        in_specs=[
            pl.BlockSpec((tm, D), lambda i, j: (i, 0)),
            pl.BlockSpec((1, D), lambda i, j: (0, 0)),
            pl.BlockSpec((D, tn), lambda i, j: (0, j)),
        ],
        out_specs=out_specs,
        out_shape=out_shape,
        scratch_shapes=[pltpu.VMEM((tm, D), BF16)],
        compiler_params=_params("parallel", "arbitrary"),
        name="norm_matmul",
    )(x, g.reshape(1, D), w)
    return res if emit_h else res[0]


def _gla_body(*refs, reverse, final, nch):
    if final:
        (q_ref, k_ref, v_ref, a_ref, wg_ref, bg_ref, oprev_ref, r_ref, ng_ref,
         o_ref, st_ref) = refs
    else:
        q_ref, k_ref, v_ref, a_ref, wg_ref, bg_ref, o_ref, st_ref = refs
    C = GLA_CHUNK

    @pl.when(pl.program_id(2) == 0)
    def _():
        st_ref[...] = jnp.zeros_like(st_ref)

    z = jnp.dot(a_ref[...], wg_ref[...], precision=lax.Precision.HIGHEST,
                preferred_element_type=F32) + bg_ref[...]
    logg = (jnp.minimum(z, 0.0) - jnp.log(1.0 + jnp.exp(-jnp.abs(z)))) * (1.0 / GLA_GATE_TAU)

    ri = lax.broadcasted_iota(jnp.int32, (C, C), 0)
    ci = lax.broadcasted_iota(jnp.int32, (C, C), 1)
    if reverse:
        tri = jnp.where(ci >= ri, 1.0, 0.0).astype(BF16)
        amask = ci > ri
    else:
        tri = jnp.where(ci <= ri, 1.0, 0.0).astype(BF16)
        amask = ci <= ri
    scale = GLA_DK ** -0.5

    order = range(nch - 1, -1, -1) if reverse else range(nch)
    for c in order:
        sl = slice(c * C, (c + 1) * C)
        lg = logg[sl, :]
        lg_hi = lg.astype(BF16)
        lg_lo = (lg - lg_hi.astype(F32)).astype(BF16)
        bcum = (jnp.dot(tri, lg_hi, preferred_element_type=F32)
                + jnp.dot(tri, lg_lo, preferred_element_type=F32))
        btot = bcum[0:1, :] if reverse else bcum[C - 1:C, :]
        q = q_ref[sl, :].astype(F32) * scale
        k = k_ref[sl, :].astype(F32)
        v = v_ref[sl, :]
        qe = (q * jnp.exp(bcum)).astype(BF16)
        ke = (k * jnp.exp(-bcum)).astype(BF16)
        kd = (k * jnp.exp(btot - bcum)).astype(BF16)
        attn = lax.dot_general(qe, ke, NT_DIMS, preferred_element_type=F32)
        attn = jnp.where(amask, attn, 0.0).astype(BF16)
        st = st_ref[...]
        o = (jnp.dot(attn, v, preferred_element_type=F32)
             + lax.dot_general(qe, st.astype(BF16), NT_DIMS, preferred_element_type=F32))
        st_ref[...] = jnp.exp(btot) * st + lax.dot_general(
            v, kd, TN_DIMS, preferred_element_type=F32)
        if final:
            o = o + oprev_ref[sl, :]
            var = jnp.mean(o * o, axis=-1, keepdims=True)
            o = o * lax.rsqrt(var + RMS_EPS) * ng_ref[...]
            r = r_ref[sl, :].astype(F32)
            o = o * (r * jax.nn.sigmoid(r))
        o_ref[sl, :] = o.astype(o_ref.dtype)


def _gla_pass(proj, a, wg_pad, bg, S, tb, reverse, oprev=None, norm_g=None):
    T = proj.shape[0]
    B = T // S
    nb = S // tb
    final = oprev is not None

    def row(b, n):
        return b * nb + ((nb - 1 - n) if reverse else n)

    kq, kk = COL_Q // GLA_DK, COL_K // GLA_DK
    kv, kr = COL_V // GLA_DV, COL_R // GLA_DV
    in_specs = [
        pl.BlockSpec((tb, GLA_DK), lambda b, h, n: (row(b, n), kq + h)),
        pl.BlockSpec((tb, GLA_DK), lambda b, h, n: (row(b, n), kk + h)),
        pl.BlockSpec((tb, GLA_DV), lambda b, h, n: (row(b, n), kv + h)),
        pl.BlockSpec((tb, GATE_PAD), lambda b, h, n: (row(b, n), 0)),
        pl.BlockSpec((GATE_PAD, GLA_DK), lambda b, h, n: (0, h)),
        pl.BlockSpec((1, GLA_DK), lambda b, h, n: (0, h)),
    ]
    args = [proj, proj, proj, a, wg_pad, bg.reshape(1, GLA_KEY_WIDTH)]
    if final:
        in_specs += [
            pl.BlockSpec((tb, GLA_DV), lambda b, h, n: (row(b, n), h)),
            pl.BlockSpec((tb, GLA_DV), lambda b, h, n: (row(b, n), kr + h)),
            pl.BlockSpec((1, GLA_DV), lambda b, h, n: (0, h)),
        ]
        args += [oprev, proj, norm_g.reshape(1, GLA_VAL_WIDTH)]
    return pl.pallas_call(
        functools.partial(_gla_body, reverse=reverse, final=final, nch=tb // GLA_CHUNK),
        grid=(B, GLA_HEADS, nb),
        in_specs=in_specs,
        out_specs=pl.BlockSpec((tb, GLA_DV), lambda b, h, n: (row(b, n), h)),
        out_shape=jax.ShapeDtypeStruct((T, GLA_VAL_WIDTH), BF16 if final else F32),
        scratch_shapes=[pltpu.VMEM((GLA_DV, GLA_DK), F32)],
        compiler_params=_params("parallel", "parallel", "arbitrary"),
        name="gla_fwd" if final else "gla_bwd",
    )(*args)


def _mix_body(y_ref, pm_ref, pp_ref, pn_ref, gg_ref, gp_ref, x_ref, wug_ref, pw_ref,
              ps_ref, wup_ref, wo_ref, o_ref, *, tm, S):
    H = POOL_HALO
    G = POOL_GROUP_DIM
    r = lax.rem(pl.program_id(0), S // tm)
    pm = pm_ref[...].astype(F32)
    pp = jnp.where(r == 0, 0.0, pp_ref[...].astype(F32))
    pn = jnp.where(r == S // tm - 1, 0.0, pn_ref[...].astype(F32))
    ext = jnp.concatenate([pp, pm, pn], axis=0)
    n = tm + 2 * H
    pos = r * tm + lax.broadcasted_iota(jnp.int32, (tm, G), 0)
    feats = []
    for gi, w in enumerate(POOL_WINDOWS):
        cols = slice(gi * G, (gi + 1) * G)
        s = ext[:, cols]
        span = 1
        while span < w:
            s = s + pltpu.roll(s, n - span, axis=0)
            span *= 2
        centred = pltpu.roll(s, w // 2, axis=0)[H:H + tm, :]
        lo = jnp.clip(pos - w // 2, 0, S)
        hi = jnp.clip(pos - w // 2 + w, 0, S)
        pooled = centred / (hi - lo).astype(F32) - pm[:, cols]
        feats.append(jnp.dot(pooled.astype(BF16), pw_ref[gi], preferred_element_type=F32))
    feat = (jnp.concatenate(feats, axis=1) * ps_ref[...]).astype(BF16)
    y_pool = jnp.dot(feat, wup_ref[...], preferred_element_type=F32)
    y_gla = jnp.dot(y_ref[...], wug_ref[...], preferred_element_type=F32)
    merged = (jax.nn.sigmoid(gg_ref[...].astype(F32)) * y_gla
              + jax.nn.sigmoid(gp_ref[...].astype(F32)) * y_pool)
    o_ref[...] = x_ref[...] + jnp.dot(merged.astype(BF16), wo_ref[...],
                                      preferred_element_type=F32)


def _mix(y, proj, x, wug, pw, ps, wup, wo, S, tm):
    T, D = x.shape
    H = POOL_HALO
    last_halo = T // H - 1
    kp = COL_P // POOL_WIDTH
    const2 = lambda i: (0, 0)
    return pl.pallas_call(
        functools.partial(_mix_body, tm=tm, S=S),
        grid=(T // tm,),
        in_specs=[
            pl.BlockSpec((tm, GLA_VAL_WIDTH), lambda i: (i, 0)),
            pl.BlockSpec((tm, POOL_WIDTH), lambda i: (i, kp)),
            pl.BlockSpec((H, POOL_WIDTH), lambda i: (jnp.maximum(i * (tm // H) - 1, 0), kp)),
            pl.BlockSpec((H, POOL_WIDTH),
                         lambda i: (jnp.minimum((i + 1) * (tm // H), last_halo), kp)),
            pl.BlockSpec((tm, D), lambda i: (i, COL_GG // D_MODEL)),
            pl.BlockSpec((tm, D), lambda i: (i, COL_GP // D_MODEL)),
            pl.BlockSpec((tm, D), lambda i: (i, 0)),
            pl.BlockSpec(wug.shape, const2),
            pl.BlockSpec(pw.shape, lambda i: (0, 0, 0)),
            pl.BlockSpec((1, POOL_WIDTH), const2),
            pl.BlockSpec(wup.shape, const2),
            pl.BlockSpec(wo.shape, const2),
        ],
        out_specs=pl.BlockSpec((tm, D), lambda i: (i, 0)),
        out_shape=jax.ShapeDtypeStruct((T, D), F32),
        compiler_params=_params("parallel"),
        name="mix_out",
    )(y, proj, proj, proj, proj, proj, x, wug, pw, ps.reshape(1, POOL_WIDTH), wup, wo)


def _split_bf16(x):
    hi = x.astype(BF16)
    return hi, (x - hi.astype(F32)).astype(BF16)


def _peer_select_body(q_ref, sub_ref, sel_ref, gate_ref, eid_ref, *, tq):
    K = PEER_TOPK
    NEG = -jnp.inf
    key_iota = lax.broadcasted_iota(jnp.int32, (PEER_NKEYS, tq), 0).astype(F32)
    k_iota = lax.broadcasted_iota(jnp.int32, (K, tq), 0)

    def scores(h, p):
        qg = q_ref[:, pl.ds(pl.multiple_of(h * 2 * PEER_HALF + p * PEER_HALF, PEER_HALF),
                            PEER_HALF)]
        q_hi, q_lo = _split_bf16(qg)
        s_hi, s_lo = _split_bf16(sub_ref[h, p])
        dot = lambda a, b: lax.dot_general(a, b, NT_DIMS, preferred_element_type=F32)
        return dot(s_hi, q_hi) + (dot(s_hi, q_lo) + dot(s_lo, q_hi))

    def top_keys(s):
        val_rows, idx_rows = [], []
        vals, idxs = jnp.zeros((K, tq), F32), jnp.zeros((K, tq), F32)
        for r in range(K):
            m = jnp.max(s, axis=0, keepdims=True)
            pos = jnp.min(jnp.where(s == m, key_iota, float(PEER_NKEYS)), axis=0, keepdims=True)
            s = jnp.where(key_iota == pos, NEG, s)
            val_rows.append(m)
            idx_rows.append(pos)
            vals = jnp.where(k_iota == r, m, vals)
            idxs = jnp.where(k_iota == r, pos, idxs)
        return val_rows, idx_rows, vals, idxs

    n_b = [K // (a + 1) for a in range(K)]
    n_rows = K + 8 * 7 + 8
    cand_iota = lax.broadcasted_iota(jnp.int32, (n_rows, tq), 0).astype(F32)
    b8 = lax.broadcasted_iota(jnp.int32, (8, tq), 0)
    nk = float(PEER_NKEYS)

    def head(h, carry):
        r1, ir1, v1, i1 = top_keys(scores(h, 0))
        r2, ir2, v2, i2 = top_keys(scores(h, 1))
        cand = [r1[0] + v2]
        eid = [ir1[0] * nk + i2]
        for a in range(1, 8):
            cand.append(jnp.where(b8 < n_b[a], r1[a] + v2[0:8, :], NEG))
            eid.append(ir1[a] * nk + i2[0:8, :])
        cand.append(v1[8:16, :] + r2[0])
        eid.append(i1[8:16, :] * nk + ir2[0])
        cand = jnp.concatenate(cand, axis=0)
        eid = jnp.concatenate(eid, axis=0)
        best = jnp.zeros((K, tq), F32)
        chosen = jnp.zeros((K, tq), F32)
        for r in range(K):
            m = jnp.max(cand, axis=0, keepdims=True)
            pos = jnp.min(jnp.where(cand == m, cand_iota, float(n_rows)), axis=0, keepdims=True)
            hit = cand_iota == pos
            e = jnp.max(jnp.where(hit, eid, -1.0), axis=0, keepdims=True)
            cand = jnp.where(hit, NEG, cand)
            best = jnp.where(k_iota == r, m, best)
            chosen = jnp.where(k_iota == r, e, chosen)
        ex = jnp.exp(best - jnp.max(best, axis=0, keepdims=True))
        gate = ex / jnp.sum(ex, axis=0, keepdims=True)
        rows = pl.ds(pl.multiple_of(h * K, K), K)
        gate_ref[rows, :] = gate
        eid_ref[rows, :] = chosen
        return carry

    lax.fori_loop(0, PEER_HEADS, head, 0, unroll=2)
    e = eid_ref[...]
    n = PEER_SLOTS
    e1 = jnp.floor(e * (1.0 / nk))
    sel_ref[:, 0:n] = e1.T
    sel_ref[:, n:2 * n] = (e - e1 * nk).T
    sel_ref[:, 2 * n:3 * n] = gate_ref[...].T


def _peer_select(q, sub, tq):
    T, QD = q.shape
    return pl.pallas_call(
        functools.partial(_peer_select_body, tq=tq),
        grid=(T // tq,),
        in_specs=[
            pl.BlockSpec((tq, QD), lambda i: (i, 0)),
            pl.BlockSpec(sub.shape, lambda i: (0, 0, 0, 0)),
        ],
        out_specs=pl.BlockSpec((tq, 3 * PEER_SLOTS), lambda i: (i, 0)),
        out_shape=jax.ShapeDtypeStruct((T, 3 * PEER_SLOTS), F32),
        scratch_shapes=[pltpu.VMEM((PEER_SLOTS, tq), F32),
                        pltpu.VMEM((PEER_SLOTS, tq), F32)],
        compiler_params=_params("parallel"),
        name="peer_select",
    )(q, sub)


GATE_GROUP = 16
GATE_HALF = GATE_GROUP // 2
GATE_PITCH = PEER_NKEYS + 8


def _peer_gates_body(sel_ref, g_ref, scr_ref, *, tg):
    n = PEER_SLOTS
    row_id = lax.broadcasted_iota(jnp.int32, (PEER_NKEYS, n), 0).astype(F32)

    def group(gi, carry):
        base = pl.multiple_of(gi * GATE_GROUP, GATE_GROUP)
        for u in range(GATE_GROUP):
            row = sel_ref[pl.ds(base + u, 1), :]
            i1, i2, gate = row[:, 0:n], row[:, n:2 * n], row[:, 2 * n:3 * n]
            p1 = jnp.where(row_id == i1, gate, 0.0).astype(BF16)
            p2 = jnp.where(row_id == i2, 1.0, 0.0).astype(BF16)
            scr_ref[u * GATE_PITCH:u * GATE_PITCH + PEER_NKEYS, :] = lax.dot_general(
                p1, p2, NT_DIMS, preferred_element_type=F32)
        for i1 in range(PEER_NKEYS):
            lo = scr_ref[pl.ds(i1, GATE_HALF, stride=GATE_PITCH), :]
            hi = scr_ref[pl.ds(GATE_HALF * GATE_PITCH + i1, GATE_HALF, stride=GATE_PITCH), :]
            g_ref[pl.ds(base, GATE_GROUP), i1 * PEER_NKEYS:(i1 + 1) * PEER_NKEYS] = (
                jnp.concatenate([lo, hi], axis=0).astype(BF16))
        return carry

    lax.fori_loop(0, tg // GATE_GROUP, group, 0)


def _peer_gates(sel, tg):
    T = sel.shape[0]
    return pl.pallas_call(
        functools.partial(_peer_gates_body, tg=tg),
        grid=(T // tg,),
        in_specs=[pl.BlockSpec((tg, 3 * PEER_SLOTS), lambda i: (i, 0))],
        out_specs=pl.BlockSpec((tg, PEER_EXPERTS), lambda i: (i, 0)),
        out_shape=jax.ShapeDtypeStruct((T, PEER_EXPERTS), BF16),
        scratch_shapes=[pltpu.VMEM((GATE_GROUP * GATE_PITCH, PEER_NKEYS), F32)],
        compiler_params=_params("parallel"),
        name="peer_gates",
    )(sel)


SQRT_HALF = np.float32(np.sqrt(0.5))


def _peer_dense_body(h_ref, g_ref, u_ref, v_ref, x_ref, o_ref):
    @pl.when(pl.program_id(1) == 0)
    def _():
        o_ref[...] = x_ref[...]

    act = lax.dot_general(h_ref[...], u_ref[...], NT_DIMS, preferred_element_type=F32)
    gelu = 0.5 * act * (1.0 + lax.erf(act * SQRT_HALF))
    w = (gelu * g_ref[...].astype(F32)).astype(BF16)
    o_ref[...] += jnp.dot(w, v_ref[...], preferred_element_type=F32)


def _peer_dense(h, gates, u, v, x, tm, te):
    T, D = x.shape
    E = u.shape[0]
    return pl.pallas_call(
        _peer_dense_body,
        grid=(T // tm, E // te),
        in_specs=[
            pl.BlockSpec((tm, D), lambda i, j: (i, 0)),
            pl.BlockSpec((tm, te), lambda i, j: (i, j)),
            pl.BlockSpec((te, D), lambda i, j: (j, 0)),
            pl.BlockSpec((te, D), lambda i, j: (j, 0)),
            pl.BlockSpec((tm, D), lambda i, j: (i, 0), pipeline_mode=pl.Buffered(1)),
        ],
        out_specs=pl.BlockSpec((tm, D), lambda i, j: (i, 0)),
        out_shape=jax.ShapeDtypeStruct((T, D), F32),
        compiler_params=_params("parallel", "arbitrary"),
        name="peer_dense",
    )(h, gates, u, v, x)


def _final_norm_body(x_ref, g_ref, o_ref):
    o_ref[...] = _rms(x_ref[...], g_ref[...])


def _final_norm(x, g, tm):
    T, D = x.shape
    return pl.pallas_call(
        _final_norm_body,
        grid=(T // tm,),
        in_specs=[pl.BlockSpec((tm, D), lambda i: (i, 0)), pl.BlockSpec((1, D), lambda i: (0, 0))],
        out_specs=pl.BlockSpec((tm, D), lambda i: (i, 0)),
        out_shape=jax.ShapeDtypeStruct((T, D), F32),
        compiler_params=_params("parallel"),
        name="final_norm",
    )(x, g.reshape(1, D))


def _split_w_in(w):
    kw, vw, rk = GLA_KEY_WIDTH, GLA_VAL_WIDTH, GLA_GATE_RANK
    a0 = 2 * kw + 2 * vw
    main = jnp.concatenate([w[:, :a0], w[:, a0 + 2 * rk:]], axis=1).astype(BF16)
    gate = jnp.pad(w[:, a0:a0 + 2 * rk], ((0, 0), (0, GATE_PAD - 2 * rk))).astype(BF16)
    return main, gate


def _pad_gate_w(wg, first_row):
    return jnp.pad(wg, ((first_row, GATE_PAD - first_row - GLA_GATE_RANK), (0, 0)))


def kernel(x, norm_mix_g, w_in, gla_gate_w_fwd, gla_gate_b_fwd, gla_gate_w_bwd, gla_gate_b_bwd, gla_norm_g, w_up_gla, pool_w, pool_scale, w_up_pool, w_out, norm_ffn_g, peer_wq, peer_subkeys, peer_u, peer_v, norm_final_g):
    B, S, D = x.shape
    T = B * S
    t = _tiles(T, S)
    xf = x.reshape(T, D)
    for l in range(DEPTH):
        w_main, w_gate = _split_w_in(w_in[l])
        proj = _norm_matmul(xf, norm_mix_g[l], w_main, BF16, t["proj_tm"], t["proj_tn"])
        a = _norm_matmul(xf, norm_mix_g[l], w_gate, F32, t["proj_tm"], t["proj_tn"])
        o_bwd = _gla_pass(proj, a, _pad_gate_w(gla_gate_w_bwd[l], GLA_GATE_RANK),
                          gla_gate_b_bwd[l], S, t["gla_tb"], reverse=True)
        y_gla = _gla_pass(proj, a, _pad_gate_w(gla_gate_w_fwd[l], 0), gla_gate_b_fwd[l],
                          S, t["gla_tb"], reverse=False, oprev=o_bwd, norm_g=gla_norm_g[l])
        xf = _mix(y_gla, proj, xf, w_up_gla[l].astype(BF16), pool_w[l].astype(BF16),
                  pool_scale[l], w_up_pool[l].astype(BF16), w_out[l].astype(BF16),
                  S, t["mix_tm"])
        q, h = _norm_matmul(xf, norm_ffn_g[l], peer_wq[l].astype(BF16), F32,
                            t["proj_tm"], t["proj_tn"], emit_h=True)
        sel = _peer_select(q, peer_subkeys[l], t["sel_tq"])
        gates = _peer_gates(sel, t["gate_tg"])
        xf = _peer_dense(h, gates, peer_u[l].astype(BF16), peer_v[l].astype(BF16), xf,
                         t["dense_tm"], t["dense_te"])
    return _final_norm(xf, norm_final_g, t["norm_tm"]).reshape(B, S, D)
```

```python
import functools

import numpy as np
import jax
import jax.numpy as jnp
from jax import lax
from jax.experimental import pallas as pl
from jax.experimental.pallas import tpu as pltpu

F32 = jnp.float32
BF16 = jnp.bfloat16

D_MODEL = 2048
DEPTH = 2
RMS_EPS = 1e-6

GLA_HEADS = 4
GLA_DK = 128
GLA_DV = 256
GLA_KEY_WIDTH = GLA_HEADS * GLA_DK
GLA_VAL_WIDTH = GLA_HEADS * GLA_DV
GLA_GATE_RANK = 16
GLA_GATE_TAU = 16.0
GLA_CHUNK = 64

POOL_WINDOWS = (2, 4, 8, 16)
POOL_WIDTH = D_MODEL // 2
POOL_GROUP_DIM = POOL_WIDTH // len(POOL_WINDOWS)
POOL_HALO = 16

PEER_HEADS = 8
PEER_NKEYS = 128
PEER_EXPERTS = PEER_NKEYS * PEER_NKEYS
PEER_HALF = 128
PEER_TOPK = 16
PEER_SLOTS = PEER_HEADS * PEER_TOPK

COL_Q = 0
COL_K = COL_Q + GLA_KEY_WIDTH
COL_V = COL_K + GLA_KEY_WIDTH
COL_R = COL_V + GLA_VAL_WIDTH
COL_P = COL_R + GLA_VAL_WIDTH
COL_GG = COL_P + POOL_WIDTH
COL_GP = COL_GG + D_MODEL
PROJ_MAIN = COL_GP + D_MODEL
GATE_PAD = 128

VMEM_LIMIT_BYTES = 56 * 1024 * 1024

NT_DIMS = (((1,), (1,)), ((), ()))
TN_DIMS = (((0,), (0,)), ((), ()))


def _tiles(T, S):
    return dict(
        proj_tm=min(1024, T), proj_tn=1024,
        gla_tb=min(512, S),
        mix_tm=min(256, S),
        sel_tq=min(128, T),
        gate_tg=min(128, T),
        dense_tm=min(1024, T), dense_te=512,
        norm_tm=min(1024, T),
    )


def _params(*sem):
    return pltpu.CompilerParams(dimension_semantics=sem, vmem_limit_bytes=VMEM_LIMIT_BYTES)


def _rms(x, g):
    var = jnp.mean(x * x, axis=-1, keepdims=True)
    return x * lax.rsqrt(var + RMS_EPS) * g


def _norm_matmul_body(x_ref, g_ref, w_ref, *out_refs, emit_h):
    if emit_h:
        o_ref, h_out_ref, h_ref = out_refs
    else:
        o_ref, h_ref = out_refs

    @pl.when(pl.program_id(1) == 0)
    def _():
        h = _rms(x_ref[...], g_ref[...]).astype(BF16)
        h_ref[...] = h
        if emit_h:
            h_out_ref[...] = h

    o_ref[...] = jnp.dot(h_ref[...], w_ref[...], preferred_element_type=F32).astype(o_ref.dtype)


def _norm_matmul(x, g, w, out_dtype, tm, tn, emit_h=False):
    T, D = x.shape
    N = w.shape[1]
    tn = min(tn, N)
    out_shape = [jax.ShapeDtypeStruct((T, N), out_dtype)]
    out_specs = [pl.BlockSpec((tm, tn), lambda i, j: (i, j))]
    if emit_h:
        out_shape.append(jax.ShapeDtypeStruct((T, D), BF16))
        out_specs.append(pl.BlockSpec((tm, D), lambda i, j: (i, 0)))
    res = pl.pallas_call(
        functools.partial(_norm_matmul_body, emit_h=emit_h),
        grid=(T // tm, N // tn),
        in_specs=[
            pl.BlockSpec((tm, D), lambda i, j: (i, 0)),
            pl.BlockSpec((1, D), lambda i, j: (0, 0)),
            pl.BlockSpec((D, tn), lambda i, j: (0, j)),
        ],
        out_specs=out_specs,
        out_shape=out_shape,
        scratch_shapes=[pltpu.VMEM((tm, D), BF16)],
        compiler_params=_params("parallel", "arbitrary"),
        name="norm_matmul",
    )(x, g.reshape(1, D), w)
    return res if emit_h else res[0]


def _gla_body(*refs, reverse, final, nch):
    if final:
        (q_ref, k_ref, v_ref, a_ref, wg_ref, bg_ref, oprev_ref, r_ref, ng_ref,
         o_ref, st_ref) = refs
    else:
        q_ref, k_ref, v_ref, a_ref, wg_ref, bg_ref, o_ref, st_ref = refs
    C, DK, DV = GLA_CHUNK, GLA_DK, GLA_DV

    @pl.when(pl.program_id(1) == 0)
    def _():
        st_ref[...] = jnp.zeros_like(st_ref)

    z = jnp.dot(a_ref[...], wg_ref[...], precision=lax.Precision.HIGHEST,
                preferred_element_type=F32) + bg_ref[...]
    logg = (jnp.minimum(z, 0.0) - jnp.log(1.0 + jnp.exp(-jnp.abs(z)))) * (1.0 / GLA_GATE_TAU)

    ri = lax.broadcasted_iota(jnp.int32, (C, C), 0)
    ci = lax.broadcasted_iota(jnp.int32, (C, C), 1)
    if reverse:
        tri = jnp.where(ci >= ri, 1.0, 0.0).astype(BF16)
        amask = ci > ri
    else:
        tri = jnp.where(ci <= ri, 1.0, 0.0).astype(BF16)
        amask = ci <= ri
    scale = DK ** -0.5

    order = range(nch - 1, -1, -1) if reverse else range(nch)
    for c in order:
        sl = slice(c * C, (c + 1) * C)
        lg = logg[sl, :]
        lg_hi = lg.astype(BF16)
        lg_lo = (lg - lg_hi.astype(F32)).astype(BF16)
        bcum_all = (jnp.dot(tri, lg_hi, preferred_element_type=F32)
                    + jnp.dot(tri, lg_lo, preferred_element_type=F32))
        for h in range(GLA_HEADS):
            kcols = slice(h * DK, (h + 1) * DK)
            vcols = slice(h * DV, (h + 1) * DV)
            bcum = bcum_all[:, kcols]
            btot = bcum[0:1, :] if reverse else bcum[C - 1:C, :]
            q = q_ref[sl, kcols].astype(F32) * scale
            k = k_ref[sl, kcols].astype(F32)
            v = v_ref[sl, vcols]
            qe = (q * jnp.exp(bcum)).astype(BF16)
            ke = (k * jnp.exp(-bcum)).astype(BF16)
            kd = (k * jnp.exp(btot - bcum)).astype(BF16)
            attn = lax.dot_general(qe, ke, NT_DIMS, preferred_element_type=F32)
            attn = jnp.where(amask, attn, 0.0).astype(BF16)
            st = st_ref[h]
            o = (jnp.dot(attn, v, preferred_element_type=F32)
                 + lax.dot_general(qe, st.astype(BF16), NT_DIMS, preferred_element_type=F32))
            st_ref[h] = jnp.exp(btot) * st + lax.dot_general(
                v, kd, TN_DIMS, preferred_element_type=F32)
            if final:
                o = o + oprev_ref[sl, vcols]
                var = jnp.mean(o * o, axis=-1, keepdims=True)
                o = o * lax.rsqrt(var + RMS_EPS) * ng_ref[:, vcols]
                r = r_ref[sl, vcols].astype(F32)
                o = o * (r * jax.nn.sigmoid(r))
            o_ref[sl, vcols] = o.astype(o_ref.dtype)


def _gla_pass(proj, a, wg_pad, bg, S, tb, reverse, oprev=None, norm_g=None):
    T = proj.shape[0]
    B = T // S
    nb = S // tb
    final = oprev is not None
    KW, VW = GLA_KEY_WIDTH, GLA_VAL_WIDTH

    def row(b, n):
        return b * nb + ((nb - 1 - n) if reverse else n)

    const = lambda b, n: (0, 0)
    in_specs = [
        pl.BlockSpec((tb, KW), lambda b, n: (row(b, n), COL_Q // KW)),
        pl.BlockSpec((tb, KW), lambda b, n: (row(b, n), COL_K // KW)),
        pl.BlockSpec((tb, VW), lambda b, n: (row(b, n), COL_V // VW)),
        pl.BlockSpec((tb, GATE_PAD), lambda b, n: (row(b, n), 0)),
        pl.BlockSpec((GATE_PAD, KW), const),
        pl.BlockSpec((1, KW), const),
    ]
    args = [proj, proj, proj, a, wg_pad, bg.reshape(1, KW)]
    if final:
        in_specs += [
            pl.BlockSpec((tb, VW), lambda b, n: (row(b, n), 0)),
            pl.BlockSpec((tb, VW), lambda b, n: (row(b, n), COL_R // VW)),
            pl.BlockSpec((1, VW), const),
        ]
        args += [oprev, proj, norm_g.reshape(1, VW)]
    return pl.pallas_call(
        functools.partial(_gla_body, reverse=reverse, final=final, nch=tb // GLA_CHUNK),
        grid=(B, nb),
        in_specs=in_specs,
        out_specs=pl.BlockSpec((tb, VW), lambda b, n: (row(b, n), 0)),
        out_shape=jax.ShapeDtypeStruct((T, VW), BF16 if final else F32),
        scratch_shapes=[pltpu.VMEM((GLA_HEADS, GLA_DV, GLA_DK), F32)],
        compiler_params=_params("parallel", "arbitrary"),
        name="gla_fwd" if final else "gla_bwd",
    )(*args)


def _mix_body(y_ref, pm_ref, pp_ref, pn_ref, gg_ref, gp_ref, x_ref, wug_ref, pw_ref,
              ps_ref, wup_ref, wo_ref, o_ref, *, tm, S):
    H = POOL_HALO
    G = POOL_GROUP_DIM
    r = lax.rem(pl.program_id(0), S // tm)
    pm = pm_ref[...].astype(F32)
    pp = jnp.where(r == 0, 0.0, pp_ref[...].astype(F32))
    pn = jnp.where(r == S // tm - 1, 0.0, pn_ref[...].astype(F32))
    ext = jnp.concatenate([pp, pm, pn], axis=0)
    n = tm + 2 * H
    pos = r * tm + lax.broadcasted_iota(jnp.int32, (tm, G), 0)
    feats = []
    for gi, w in enumerate(POOL_WINDOWS):
        cols = slice(gi * G, (gi + 1) * G)
        s = ext[:, cols]
        span = 1
        while span < w:
            s = s + pltpu.roll(s, n - span, axis=0)
            span *= 2
        centred = pltpu.roll(s, w // 2, axis=0)[H:H + tm, :]
        lo = jnp.clip(pos - w // 2, 0, S)
        hi = jnp.clip(pos - w // 2 + w, 0, S)
        pooled = centred / (hi - lo).astype(F32) - pm[:, cols]
        feats.append(jnp.dot(pooled.astype(BF16), pw_ref[gi], preferred_element_type=F32))
    feat = (jnp.concatenate(feats, axis=1) * ps_ref[...]).astype(BF16)
    y_pool = jnp.dot(feat, wup_ref[...], preferred_element_type=F32)
    y_gla = jnp.dot(y_ref[...], wug_ref[...], preferred_element_type=F32)
    merged = (jax.nn.sigmoid(gg_ref[...].astype(F32)) * y_gla
              + jax.nn.sigmoid(gp_ref[...].astype(F32)) * y_pool)
    o_ref[...] = x_ref[...] + jnp.dot(merged.astype(BF16), wo_ref[...],
                                      preferred_element_type=F32)


def _mix(y, proj, x, wug, pw, ps, wup, wo, S, tm):
    T, D = x.shape
    H = POOL_HALO
    last_halo = T // H - 1
    kp = COL_P // POOL_WIDTH
    const2 = lambda i: (0, 0)
    return pl.pallas_call(
        functools.partial(_mix_body, tm=tm, S=S),
        grid=(T // tm,),
        in_specs=[
            pl.BlockSpec((tm, GLA_VAL_WIDTH), lambda i: (i, 0)),
            pl.BlockSpec((tm, POOL_WIDTH), lambda i: (i, kp)),
            pl.BlockSpec((H, POOL_WIDTH), lambda i: (jnp.maximum(i * (tm // H) - 1, 0), kp)),
            pl.BlockSpec((H, POOL_WIDTH),
                         lambda i: (jnp.minimum((i + 1) * (tm // H), last_halo), kp)),
            pl.BlockSpec((tm, D), lambda i: (i, COL_GG // D_MODEL)),
            pl.BlockSpec((tm, D), lambda i: (i, COL_GP // D_MODEL)),
            pl.BlockSpec((tm, D), lambda i: (i, 0)),
            pl.BlockSpec(wug.shape, const2),
            pl.BlockSpec(pw.shape, lambda i: (0, 0, 0)),
            pl.BlockSpec((1, POOL_WIDTH), const2),
            pl.BlockSpec(wup.shape, const2),
            pl.BlockSpec(wo.shape, const2),
        ],
        out_specs=pl.BlockSpec((tm, D), lambda i: (i, 0)),
        out_shape=jax.ShapeDtypeStruct((T, D), F32),
        compiler_params=_params("parallel"),
        name="mix_out",
    )(y, proj, proj, proj, proj, proj, x, wug, pw, ps.reshape(1, POOL_WIDTH), wup, wo)


def _split_bf16(x):
    hi = x.astype(BF16)
    return hi, (x - hi.astype(F32)).astype(BF16)


def _sort_network(n):
    pairs, p = [], 1
    while p < n:
        k = p
        while k >= 1:
            for j in range(k % p, n - k, 2 * k):
                for i in range(min(k, n - j - k)):
                    if (i + j) // (2 * p) == (i + j + k) // (2 * p):
                        pairs.append((i + j, i + j + k))
            k //= 2
        p *= 2
    return pairs


F32_SUBLANES = 8


def _peer_select_body(q_ref, sub_ref, sel_ref, gate_ref, i1_ref, i2_ref, *, tq):
    K = PEER_TOPK
    SUB = F32_SUBLANES
    NEG = -jnp.inf
    BIG = 1e9
    n_vreg = PEER_NKEYS // SUB
    network = _sort_network(n_vreg)
    sub_id = lax.broadcasted_iota(jnp.int32, (SUB, tq), 0)
    sub_f = sub_id.astype(F32)
    k_iota = lax.broadcasted_iota(jnp.int32, (K, tq), 0)

    def stack_rows(rows):
        acc = jnp.broadcast_to(jnp.asarray(rows[-1], F32), (SUB, tq))
        for k in range(SUB - 2, -1, -1):
            acc = jnp.where(sub_id == k, rows[k], acc)
        return acc

    def scores(h, p):
        qg = q_ref[:, pl.ds(pl.multiple_of(h * 2 * PEER_HALF + p * PEER_HALF, PEER_HALF),
                            PEER_HALF)]
        q_hi, q_lo = _split_bf16(qg)
        s_hi, s_lo = _split_bf16(sub_ref[h, p])
        dot = lambda a, b: lax.dot_general(a, b, NT_DIMS, preferred_element_type=F32)
        return dot(s_hi, q_hi) + (dot(s_hi, q_lo) + dot(s_lo, q_hi))

    def top_keys(s):
        val = [s[v * SUB:(v + 1) * SUB, :] for v in range(n_vreg)]
        idx = [sub_f + float(v * SUB) for v in range(n_vreg)]
        for i, j in network:
            a, ia, b, ib = val[i], idx[i], val[j], idx[j]
            swap = (b > a) | ((b == a) & (ib < ia))
            val[i], val[j] = jnp.maximum(a, b), jnp.minimum(a, b)
            idx[i], idx[j] = jnp.where(swap, ib, ia), jnp.where(swap, ia, ib)
        val_rows, idx_rows = [], []
        for r in range(K):
            m = jnp.max(val[0], axis=0, keepdims=True)
            pos = jnp.min(jnp.where(val[0] == m, idx[0], BIG), axis=0, keepdims=True)
            val_rows.append(m)
            idx_rows.append(pos)
            hit = idx[0] == pos
            for i in range(K - 1 - r):
                val[i] = jnp.where(hit, val[i + 1], val[i])
                idx[i] = jnp.where(hit, idx[i + 1], idx[i])
        return val_rows, idx_rows

    packed = [(a, b) for a in range(2, K) for b in range(K // (a + 1))]
    packed += [None] * (-len(packed) % SUB)
    groups = [packed[g:g + SUB] for g in range(0, len(packed), SUB)]
    flat_id = [sub_f, sub_f + float(SUB), sub_f + float(K)]
    flat_id += [stack_rows([BIG if ab is None else float(ab[0] * K + ab[1]) for ab in grp])
                for grp in groups]

    def head(h, carry):
        r1, ir1 = top_keys(scores(h, 0))
        r2, ir2 = top_keys(scores(h, 1))
        v2_lo, v2_hi = stack_rows(r2[:SUB]), stack_rows(r2[SUB:])
        cand = [r1[0] + v2_lo, r1[0] + v2_hi, r1[1] + v2_lo]
        for grp in groups:
            cand.append(stack_rows([NEG if ab is None else r1[ab[0]] for ab in grp])
                        + stack_rows([0.0 if ab is None else r2[ab[1]] for ab in grp]))
        best = jnp.zeros((K, tq), F32)
        flat = jnp.zeros((K, tq), F32)
        for r in range(K):
            m = jnp.max(functools.reduce(jnp.maximum, cand), axis=0, keepdims=True)
            pos = jnp.min(functools.reduce(
                jnp.minimum, [jnp.where(c == m, f, BIG) for c, f in zip(cand, flat_id)]),
                axis=0, keepdims=True)
            cand = [jnp.where(f == pos, NEG, c) for c, f in zip(cand, flat_id)]
            best = jnp.where(k_iota == r, m, best)
            flat = jnp.where(k_iota == r, pos, flat)
        a_sel = jnp.floor(flat * (1.0 / K))
        b_sel = flat - a_sel * K
        i1 = jnp.zeros((K, tq), F32)
        i2 = jnp.zeros((K, tq), F32)
        for k in range(K):
            i1 = jnp.where(a_sel == k, ir1[k], i1)
            i2 = jnp.where(b_sel == k, ir2[k], i2)
        ex = jnp.exp(best - jnp.max(best, axis=0, keepdims=True))
        rows = pl.ds(pl.multiple_of(h * K, K), K)
        gate_ref[rows, :] = ex / jnp.sum(ex, axis=0, keepdims=True)
        i1_ref[rows, :] = i1
        i2_ref[rows, :] = i2
        return carry

    lax.fori_loop(0, PEER_HEADS, head, 0, unroll=True)
    n = PEER_SLOTS
    sel_ref[:, 0:n] = i1_ref[...].T
    sel_ref[:, n:2 * n] = i2_ref[...].T
    sel_ref[:, 2 * n:3 * n] = gate_ref[...].T


def _peer_select(q, sub, tq):
    T, QD = q.shape
    return pl.pallas_call(
        functools.partial(_peer_select_body, tq=tq),
        grid=(T // tq,),
        in_specs=[
            pl.BlockSpec((tq, QD), lambda i: (i, 0)),
            pl.BlockSpec(sub.shape, lambda i: (0, 0, 0, 0)),
        ],
        out_specs=pl.BlockSpec((tq, 3 * PEER_SLOTS), lambda i: (i, 0)),
        out_shape=jax.ShapeDtypeStruct((T, 3 * PEER_SLOTS), F32),
        scratch_shapes=[pltpu.VMEM((PEER_SLOTS, tq), F32)] * 3,
        compiler_params=_params("parallel"),
        name="peer_select",
    )(q, sub)


GATE_GROUP = 16
GATE_HALF = GATE_GROUP // 2
GATE_PITCH = PEER_NKEYS + 8


def _peer_gates_body(sel_ref, g_ref, scr_ref, *, tg):
    n = PEER_SLOTS
    row_id = lax.broadcasted_iota(jnp.int32, (PEER_NKEYS, n), 0).astype(F32)

    def group(gi, carry):
        base = pl.multiple_of(gi * GATE_GROUP, GATE_GROUP)
        for u in range(GATE_GROUP):
            row = sel_ref[pl.ds(base + u, 1), :]
            i1, i2, gate = row[:, 0:n], row[:, n:2 * n], row[:, 2 * n:3 * n]
            p1 = jnp.where(row_id == i1, gate, 0.0).astype(BF16)
            p2 = jnp.where(row_id == i2, 1.0, 0.0).astype(BF16)
            scr_ref[u * GATE_PITCH:u * GATE_PITCH + PEER_NKEYS, :] = lax.dot_general(
                p1, p2, NT_DIMS, preferred_element_type=F32)
        for i1 in range(PEER_NKEYS):
            lo = scr_ref[pl.ds(i1, GATE_HALF, stride=GATE_PITCH), :]
            hi = scr_ref[pl.ds(GATE_HALF * GATE_PITCH + i1, GATE_HALF, stride=GATE_PITCH), :]
            g_ref[pl.ds(base, GATE_GROUP), i1 * PEER_NKEYS:(i1 + 1) * PEER_NKEYS] = (
                jnp.concatenate([lo, hi], axis=0).astype(BF16))
        return carry

    lax.fori_loop(0, tg // GATE_GROUP, group, 0)


def _peer_gates(sel, tg):
    T = sel.shape[0]
    return pl.pallas_call(
        functools.partial(_peer_gates_body, tg=tg),
        grid=(T // tg,),
        in_specs=[pl.BlockSpec((tg, 3 * PEER_SLOTS), lambda i: (i, 0))],
        out_specs=pl.BlockSpec((tg, PEER_EXPERTS), lambda i: (i, 0)),
        out_shape=jax.ShapeDtypeStruct((T, PEER_EXPERTS), BF16),
        scratch_shapes=[pltpu.VMEM((GATE_GROUP * GATE_PITCH, PEER_NKEYS), F32)],
        compiler_params=_params("parallel"),
        name="peer_gates",
    )(sel)


SQRT_HALF = np.float32(np.sqrt(0.5))


def _peer_dense_body(h_ref, g_ref, u_ref, v_ref, x_ref, o_ref):
    @pl.when(pl.program_id(1) == 0)
    def _():
        o_ref[...] = x_ref[...]

    act = lax.dot_general(h_ref[...], u_ref[...], NT_DIMS, preferred_element_type=F32)
    gelu = 0.5 * act * (1.0 + lax.erf(act * SQRT_HALF))
    w = (gelu * g_ref[...].astype(F32)).astype(BF16)
    o_ref[...] += jnp.dot(w, v_ref[...], preferred_element_type=F32)


def _peer_dense(h, gates, u, v, x, tm, te):
    T, D = x.shape
    E = u.shape[0]
    return pl.pallas_call(
        _peer_dense_body,
        grid=(T // tm, E // te),
        in_specs=[
            pl.BlockSpec((tm, D), lambda i, j: (i, 0)),
            pl.BlockSpec((tm, te), lambda i, j: (i, j)),
            pl.BlockSpec((te, D), lambda i, j: (j, 0)),
            pl.BlockSpec((te, D), lambda i, j: (j, 0)),
            pl.BlockSpec((tm, D), lambda i, j: (i, 0), pipeline_mode=pl.Buffered(1)),
        ],
        out_specs=pl.BlockSpec((tm, D), lambda i, j: (i, 0)),
        out_shape=jax.ShapeDtypeStruct((T, D), F32),
        compiler_params=_params("parallel", "arbitrary"),
        name="peer_dense",
    )(h, gates, u, v, x)


def _final_norm_body(x_ref, g_ref, o_ref):
    o_ref[...] = _rms(x_ref[...], g_ref[...])


def _final_norm(x, g, tm):
    T, D = x.shape
    return pl.pallas_call(
        _final_norm_body,
        grid=(T // tm,),
        in_specs=[pl.BlockSpec((tm, D), lambda i: (i, 0)), pl.BlockSpec((1, D), lambda i: (0, 0))],
        out_specs=pl.BlockSpec((tm, D), lambda i: (i, 0)),
        out_shape=jax.ShapeDtypeStruct((T, D), F32),
        compiler_params=_params("parallel"),
        name="final_norm",
    )(x, g.reshape(1, D))


def _split_w_in(w):
    kw, vw, rk = GLA_KEY_WIDTH, GLA_VAL_WIDTH, GLA_GATE_RANK
    a0 = 2 * kw + 2 * vw
    main = jnp.concatenate([w[:, :a0], w[:, a0 + 2 * rk:]], axis=1).astype(BF16)
    gate = jnp.pad(w[:, a0:a0 + 2 * rk], ((0, 0), (0, GATE_PAD - 2 * rk))).astype(BF16)
    return main, gate


def _pad_gate_w(wg, first_row):
    return jnp.pad(wg, ((first_row, GATE_PAD - first_row - GLA_GATE_RANK), (0, 0)))


def kernel(x, norm_mix_g, w_in, gla_gate_w_fwd, gla_gate_b_fwd, gla_gate_w_bwd, gla_gate_b_bwd, gla_norm_g, w_up_gla, pool_w, pool_scale, w_up_pool, w_out, norm_ffn_g, peer_wq, peer_subkeys, peer_u, peer_v, norm_final_g):
    B, S, D = x.shape
    T = B * S
    t = _tiles(T, S)
    xf = x.reshape(T, D)
    for l in range(DEPTH):
        w_main, w_gate = _split_w_in(w_in[l])
        proj = _norm_matmul(xf, norm_mix_g[l], w_main, BF16, t["proj_tm"], t["proj_tn"])
        a = _norm_matmul(xf, norm_mix_g[l], w_gate, F32, t["proj_tm"], t["proj_tn"])
        o_bwd = _gla_pass(proj, a, _pad_gate_w(gla_gate_w_bwd[l], GLA_GATE_RANK),
                          gla_gate_b_bwd[l], S, t["gla_tb"], reverse=True)
        y_gla = _gla_pass(proj, a, _pad_gate_w(gla_gate_w_fwd[l], 0), gla_gate_b_fwd[l],
                          S, t["gla_tb"], reverse=False, oprev=o_bwd, norm_g=gla_norm_g[l])
        xf = _mix(y_gla, proj, xf, w_up_gla[l].astype(BF16), pool_w[l].astype(BF16),
                  pool_scale[l], w_up_pool[l].astype(BF16), w_out[l].astype(BF16),
                  S, t["mix_tm"])
        q, h = _norm_matmul(xf, norm_ffn_g[l], peer_wq[l].astype(BF16), F32,
                            t["proj_tm"], t["proj_tn"], emit_h=True)
        sel = _peer_select(q, peer_subkeys[l], t["sel_tq"])
        gates = _peer_gates(sel, t["gate_tg"])
        xf = _peer_dense(h, gates, peer_u[l].astype(BF16), peer_v[l].astype(BF16), xf,
                         t["dense_tm"], t["dense_te"])
    return _final_norm(xf, norm_final_g, t["norm_tm"]).reshape(B, S, D)
```

```python
import functools

import numpy as np
import jax
import jax.numpy as jnp
from jax import lax
from jax.experimental import pallas as pl
from jax.experimental.pallas import tpu as pltpu

F32 = jnp.float32
BF16 = jnp.bfloat16

D_MODEL = 2048
DEPTH = 2
RMS_EPS = 1e-6

GLA_HEADS = 4
GLA_DK = 128
GLA_DV = 256
GLA_KEY_WIDTH = GLA_HEADS * GLA_DK
GLA_VAL_WIDTH = GLA_HEADS * GLA_DV
GLA_GATE_RANK = 16
GLA_GATE_TAU = 16.0
GLA_CHUNK = 64

POOL_WINDOWS = (2, 4, 8, 16)
POOL_WIDTH = D_MODEL // 2
POOL_GROUP_DIM = POOL_WIDTH // len(POOL_WINDOWS)
POOL_HALO = 16

PEER_HEADS = 8
PEER_NKEYS = 128
PEER_EXPERTS = PEER_NKEYS * PEER_NKEYS
PEER_HALF = 128
PEER_TOPK = 16
PEER_SLOTS = PEER_HEADS * PEER_TOPK

COL_Q = 0
COL_K = COL_Q + GLA_KEY_WIDTH
COL_V = COL_K + GLA_KEY_WIDTH
COL_R = COL_V + GLA_VAL_WIDTH
COL_P = COL_R + GLA_VAL_WIDTH
COL_GG = COL_P + POOL_WIDTH
COL_GP = COL_GG + D_MODEL
PROJ_MAIN = COL_GP + D_MODEL
GATE_PAD = 128

VMEM_LIMIT_BYTES = 56 * 1024 * 1024

NT_DIMS = (((1,), (1,)), ((), ()))
TN_DIMS = (((0,), (0,)), ((), ()))


def _tiles(T, S):
    return dict(
        proj_tm=min(1024, T), proj_tn=2048,
        gla_tb=min(512, S),
        mix_tm=min(256, S),
        sel_tq=min(128, T),
        gate_tg=min(128, T),
        dense_tm=min(1024, T), dense_te=1024,
    )


def _params(*sem):
    return pltpu.CompilerParams(dimension_semantics=sem, vmem_limit_bytes=VMEM_LIMIT_BYTES)


def _rms(x, g):
    var = jnp.mean(x * x, axis=-1, keepdims=True)
    return x * lax.rsqrt(var + RMS_EPS) * g


def _norm_matmul_body(x_ref, g_ref, w_ref, *out_refs, emit_h):
    if emit_h:
        o_ref, h_out_ref, h_ref = out_refs
    else:
        o_ref, h_ref = out_refs

    @pl.when(pl.program_id(1) == 0)
    def _():
        h = _rms(x_ref[...], g_ref[...]).astype(BF16)
        h_ref[...] = h
        if emit_h:
            h_out_ref[...] = h

    o_ref[...] = jnp.dot(h_ref[...], w_ref[...], preferred_element_type=F32).astype(o_ref.dtype)


def _norm_matmul(x, g, w, out_dtype, tm, tn, emit_h=False):
    T, D = x.shape
    N = w.shape[1]
    tn = min(tn, N)
    out_shape = [jax.ShapeDtypeStruct((T, N), out_dtype)]
    out_specs = [pl.BlockSpec((tm, tn), lambda i, j: (i, j))]
    if emit_h:
        out_shape.append(jax.ShapeDtypeStruct((T, D), BF16))
        out_specs.append(pl.BlockSpec((tm, D), lambda i, j: (i, 0)))
    res = pl.pallas_call(
        functools.partial(_norm_matmul_body, emit_h=emit_h),
        grid=(T // tm, N // tn),
        in_specs=[
            pl.BlockSpec((tm, D), lambda i, j: (i, 0)),
            pl.BlockSpec((1, D), lambda i, j: (0, 0)),
            pl.BlockSpec((D, tn), lambda i, j: (0, j)),
        ],
        out_specs=out_specs,
        out_shape=out_shape,
        scratch_shapes=[pltpu.VMEM((tm, D), BF16)],
        compiler_params=_params("parallel", "arbitrary"),
        name="norm_matmul",
    )(x, g.reshape(1, D), w)
    return res if emit_h else res[0]


def _gla_body(*refs, reverse, final, nch):
    if final:
        (q_ref, k_ref, v_ref, a_ref, wg_ref, bg_ref, oprev_ref, r_ref, ng_ref,
         o_ref, st_ref) = refs
    else:
        q_ref, k_ref, v_ref, a_ref, wg_ref, bg_ref, o_ref, st_ref = refs
    C, DK, DV = GLA_CHUNK, GLA_DK, GLA_DV

    @pl.when(pl.program_id(1) == 0)
    def _():
        st_ref[...] = jnp.zeros_like(st_ref)

    z = jnp.dot(a_ref[...], wg_ref[...], precision=lax.Precision.HIGHEST,
                preferred_element_type=F32) + bg_ref[...]
    logg = (jnp.minimum(z, 0.0) - jnp.log(1.0 + jnp.exp(-jnp.abs(z)))) * (1.0 / GLA_GATE_TAU)

    ri = lax.broadcasted_iota(jnp.int32, (C, C), 0)
    ci = lax.broadcasted_iota(jnp.int32, (C, C), 1)
    if reverse:
        tri = jnp.where(ci >= ri, 1.0, 0.0).astype(BF16)
        amask = ci > ri
    else:
        tri = jnp.where(ci <= ri, 1.0, 0.0).astype(BF16)
        amask = ci <= ri
    scale = DK ** -0.5

    order = range(nch - 1, -1, -1) if reverse else range(nch)
    for c in order:
        sl = slice(c * C, (c + 1) * C)
        lg = logg[sl, :]
        lg_hi = lg.astype(BF16)
        lg_lo = (lg - lg_hi.astype(F32)).astype(BF16)
        bcum_all = (jnp.dot(tri, lg_hi, preferred_element_type=F32)
                    + jnp.dot(tri, lg_lo, preferred_element_type=F32))
        for h in range(GLA_HEADS):
            kcols = slice(h * DK, (h + 1) * DK)
            vcols = slice(h * DV, (h + 1) * DV)
            bcum = bcum_all[:, kcols]
            btot = bcum[0:1, :] if reverse else bcum[C - 1:C, :]
            q = q_ref[sl, kcols].astype(F32) * scale
            k = k_ref[sl, kcols].astype(F32)
            v = v_ref[sl, vcols]
            qe = (q * jnp.exp(bcum)).astype(BF16)
            ke = (k * jnp.exp(-bcum)).astype(BF16)
            kd = (k * jnp.exp(btot - bcum)).astype(BF16)
            attn = lax.dot_general(qe, ke, NT_DIMS, preferred_element_type=F32)
            attn = jnp.where(amask, attn, 0.0).astype(BF16)
            st = st_ref[h]
            o = (jnp.dot(attn, v, preferred_element_type=F32)
                 + lax.dot_general(qe, st.astype(BF16), NT_DIMS, preferred_element_type=F32))
            st_ref[h] = jnp.exp(btot) * st + lax.dot_general(
                v, kd, TN_DIMS, preferred_element_type=F32)
            if final:
                o = o + oprev_ref[sl, vcols]
                var = jnp.mean(o * o, axis=-1, keepdims=True)
                o = o * lax.rsqrt(var + RMS_EPS) * ng_ref[:, vcols]
                r = r_ref[sl, vcols].astype(F32)
                o = o * (r * jax.nn.sigmoid(r))
            o_ref[sl, vcols] = o.astype(o_ref.dtype)


def _gla_pass(proj, a, wg_pad, bg, S, tb, reverse, oprev=None, norm_g=None):
    T = proj.shape[0]
    B = T // S
    nb = S // tb
    final = oprev is not None
    KW, VW = GLA_KEY_WIDTH, GLA_VAL_WIDTH

    def row(b, n):
        return b * nb + ((nb - 1 - n) if reverse else n)

    const = lambda b, n: (0, 0)
    in_specs = [
        pl.BlockSpec((tb, KW), lambda b, n: (row(b, n), COL_Q // KW)),
        pl.BlockSpec((tb, KW), lambda b, n: (row(b, n), COL_K // KW)),
        pl.BlockSpec((tb, VW), lambda b, n: (row(b, n), COL_V // VW)),
        pl.BlockSpec((tb, GATE_PAD), lambda b, n: (row(b, n), 0)),
        pl.BlockSpec((GATE_PAD, KW), const),
        pl.BlockSpec((1, KW), const),
    ]
    args = [proj, proj, proj, a, wg_pad, bg.reshape(1, KW)]
    if final:
        in_specs += [
            pl.BlockSpec((tb, VW), lambda b, n: (row(b, n), 0)),
            pl.BlockSpec((tb, VW), lambda b, n: (row(b, n), COL_R // VW)),
            pl.BlockSpec((1, VW), const),
        ]
        args += [oprev, proj, norm_g.reshape(1, VW)]
    return pl.pallas_call(
        functools.partial(_gla_body, reverse=reverse, final=final, nch=tb // GLA_CHUNK),
        grid=(B, nb),
        in_specs=in_specs,
        out_specs=pl.BlockSpec((tb, VW), lambda b, n: (row(b, n), 0)),
        out_shape=jax.ShapeDtypeStruct((T, VW), BF16 if final else F32),
        scratch_shapes=[pltpu.VMEM((GLA_HEADS, GLA_DV, GLA_DK), F32)],
        compiler_params=_params("parallel", "arbitrary"),
        name="gla_fwd" if final else "gla_bwd",
    )(*args)


def _mix_body(y_ref, pm_ref, pp_ref, pn_ref, gg_ref, gp_ref, x_ref, wug_ref, pw_ref,
              ps_ref, wup_ref, wo_ref, o_ref, *, tm, S):
    H = POOL_HALO
    G = POOL_GROUP_DIM
    r = lax.rem(pl.program_id(0), S // tm)
    pm = pm_ref[...].astype(F32)
    pp = jnp.where(r == 0, 0.0, pp_ref[...].astype(F32))
    pn = jnp.where(r == S // tm - 1, 0.0, pn_ref[...].astype(F32))
    ext = jnp.concatenate([pp, pm, pn], axis=0)
    n = tm + 2 * H
    pos = r * tm + lax.broadcasted_iota(jnp.int32, (tm, G), 0)
    feats = []
    for gi, w in enumerate(POOL_WINDOWS):
        cols = slice(gi * G, (gi + 1) * G)
        s = ext[:, cols]
        span = 1
        while span < w:
            s = s + pltpu.roll(s, n - span, axis=0)
            span *= 2
        centred = pltpu.roll(s, w // 2, axis=0)[H:H + tm, :]
        lo = jnp.clip(pos - w // 2, 0, S)
        hi = jnp.clip(pos - w // 2 + w, 0, S)
        pooled = centred / (hi - lo).astype(F32) - pm[:, cols]
        feats.append(jnp.dot(pooled.astype(BF16), pw_ref[gi], preferred_element_type=F32))
    feat = (jnp.concatenate(feats, axis=1) * ps_ref[...]).astype(BF16)
    y_pool = jnp.dot(feat, wup_ref[...], preferred_element_type=F32)
    y_gla = jnp.dot(y_ref[...], wug_ref[...], preferred_element_type=F32)
    merged = (jax.nn.sigmoid(gg_ref[...].astype(F32)) * y_gla
              + jax.nn.sigmoid(gp_ref[...].astype(F32)) * y_pool)
    o_ref[...] = x_ref[...] + jnp.dot(merged.astype(BF16), wo_ref[...],
                                      preferred_element_type=F32)


def _mix(y, proj, x, wug, pw, ps, wup, wo, S, tm):
    T, D = x.shape
    H = POOL_HALO
    last_halo = T // H - 1
    kp = COL_P // POOL_WIDTH
    const2 = lambda i: (0, 0)
    return pl.pallas_call(
        functools.partial(_mix_body, tm=tm, S=S),
        grid=(T // tm,),
        in_specs=[
            pl.BlockSpec((tm, GLA_VAL_WIDTH), lambda i: (i, 0)),
            pl.BlockSpec((tm, POOL_WIDTH), lambda i: (i, kp)),
            pl.BlockSpec((H, POOL_WIDTH), lambda i: (jnp.maximum(i * (tm // H) - 1, 0), kp)),
            pl.BlockSpec((H, POOL_WIDTH),
                         lambda i: (jnp.minimum((i + 1) * (tm // H), last_halo), kp)),
            pl.BlockSpec((tm, D), lambda i: (i, COL_GG // D_MODEL)),
            pl.BlockSpec((tm, D), lambda i: (i, COL_GP // D_MODEL)),
            pl.BlockSpec((tm, D), lambda i: (i, 0)),
            pl.BlockSpec(wug.shape, const2),
            pl.BlockSpec(pw.shape, lambda i: (0, 0, 0)),
            pl.BlockSpec((1, POOL_WIDTH), const2),
            pl.BlockSpec(wup.shape, const2),
            pl.BlockSpec(wo.shape, const2),
        ],
        out_specs=pl.BlockSpec((tm, D), lambda i: (i, 0)),
        out_shape=jax.ShapeDtypeStruct((T, D), F32),
        compiler_params=_params("parallel"),
        name="mix_out",
    )(y, proj, proj, proj, proj, proj, x, wug, pw, ps.reshape(1, POOL_WIDTH), wup, wo)


def _split_bf16(x):
    hi = x.astype(BF16)
    return hi, (x - hi.astype(F32)).astype(BF16)


def _sort_network(n):
    pairs, p = [], 1
    while p < n:
        k = p
        while k >= 1:
            for j in range(k % p, n - k, 2 * k):
                for i in range(min(k, n - j - k)):
                    if (i + j) // (2 * p) == (i + j + k) // (2 * p):
                        pairs.append((i + j, i + j + k))
            k //= 2
        p *= 2
    return pairs


F32_SUBLANES = 8


def _peer_select_body(q_ref, sub_ref, sel_ref, gate_ref, i1_ref, i2_ref, *, tq):
    K = PEER_TOPK
    SUB = F32_SUBLANES
    NEG = -jnp.inf
    BIG = 1e9
    n_vreg = PEER_NKEYS // SUB
    network = _sort_network(n_vreg)
    sub_id = lax.broadcasted_iota(jnp.int32, (SUB, tq), 0)
    sub_f = sub_id.astype(F32)
    k_iota = lax.broadcasted_iota(jnp.int32, (K, tq), 0)

    def stack_rows(rows):
        acc = jnp.broadcast_to(jnp.asarray(rows[-1], F32), (SUB, tq))
        for k in range(SUB - 2, -1, -1):
            acc = jnp.where(sub_id == k, rows[k], acc)
        return acc

    def scores(h, p):
        qg = q_ref[:, pl.ds(pl.multiple_of(h * 2 * PEER_HALF + p * PEER_HALF, PEER_HALF),
                            PEER_HALF)]
        q_hi, q_lo = _split_bf16(qg)
        s_hi, s_lo = _split_bf16(sub_ref[h, p])
        dot = lambda a, b: lax.dot_general(a, b, NT_DIMS, preferred_element_type=F32)
        return dot(s_hi, q_hi) + (dot(s_hi, q_lo) + dot(s_lo, q_hi))

    def top_keys(s):
        val = [s[v * SUB:(v + 1) * SUB, :] for v in range(n_vreg)]
        idx = [sub_f + float(v * SUB) for v in range(n_vreg)]
        for i, j in network:
            a, ia, b, ib = val[i], idx[i], val[j], idx[j]
            swap = (b > a) | ((b == a) & (ib < ia))
            val[i], val[j] = jnp.maximum(a, b), jnp.minimum(a, b)
            idx[i], idx[j] = jnp.where(swap, ib, ia), jnp.where(swap, ia, ib)
        val_rows, idx_rows = [], []
        for r in range(K):
            m = jnp.max(val[0], axis=0, keepdims=True)
            pos = jnp.min(jnp.where(val[0] == m, idx[0], BIG), axis=0, keepdims=True)
            val_rows.append(m)
            idx_rows.append(pos)
            hit = idx[0] == pos
            for i in range(K - 1 - r):
                val[i] = jnp.where(hit, val[i + 1], val[i])
                idx[i] = jnp.where(hit, idx[i + 1], idx[i])
        return val_rows, idx_rows

    packed = [(a, b) for a in range(2, K) for b in range(K // (a + 1))]
    packed += [None] * (-len(packed) % SUB)
    groups = [packed[g:g + SUB] for g in range(0, len(packed), SUB)]
    flat_id = [sub_f, sub_f + float(SUB), sub_f + float(K)]
    flat_id += [stack_rows([BIG if ab is None else float(ab[0] * K + ab[1]) for ab in grp])
                for grp in groups]

    def head(h, carry):
        r1, ir1 = top_keys(scores(h, 0))
        r2, ir2 = top_keys(scores(h, 1))
        v2_lo, v2_hi = stack_rows(r2[:SUB]), stack_rows(r2[SUB:])
        cand = [r1[0] + v2_lo, r1[0] + v2_hi, r1[1] + v2_lo]
        for grp in groups:
            cand.append(stack_rows([NEG if ab is None else r1[ab[0]] for ab in grp])
                        + stack_rows([0.0 if ab is None else r2[ab[1]] for ab in grp]))
        best = jnp.zeros((K, tq), F32)
        flat = jnp.zeros((K, tq), F32)
        for r in range(K):
            m = jnp.max(functools.reduce(jnp.maximum, cand), axis=0, keepdims=True)
            pos = jnp.min(functools.reduce(
                jnp.minimum, [jnp.where(c == m, f, BIG) for c, f in zip(cand, flat_id)]),
                axis=0, keepdims=True)
            cand = [jnp.where(f == pos, NEG, c) for c, f in zip(cand, flat_id)]
            best = jnp.where(k_iota == r, m, best)
            flat = jnp.where(k_iota == r, pos, flat)
        a_sel = jnp.floor(flat * (1.0 / K))
        b_sel = flat - a_sel * K
        i1 = jnp.zeros((K, tq), F32)
        i2 = jnp.zeros((K, tq), F32)
        for k in range(K):
            i1 = jnp.where(a_sel == k, ir1[k], i1)
            i2 = jnp.where(b_sel == k, ir2[k], i2)
        ex = jnp.exp(best - jnp.max(best, axis=0, keepdims=True))
        rows = pl.ds(pl.multiple_of(h * K, K), K)
        gate_ref[rows, :] = ex / jnp.sum(ex, axis=0, keepdims=True)
        i1_ref[rows, :] = i1
        i2_ref[rows, :] = i2
        return carry

    lax.fori_loop(0, PEER_HEADS, head, 0, unroll=True)
    n = PEER_SLOTS
    sel_ref[:, 0:n] = i1_ref[...].T
    sel_ref[:, n:2 * n] = i2_ref[...].T
    sel_ref[:, 2 * n:3 * n] = gate_ref[...].T


def _peer_select(q, sub, tq):
    T, QD = q.shape
    return pl.pallas_call(
        functools.partial(_peer_select_body, tq=tq),
        grid=(T // tq,),
        in_specs=[
            pl.BlockSpec((tq, QD), lambda i: (i, 0)),
            pl.BlockSpec(sub.shape, lambda i: (0, 0, 0, 0)),
        ],
        out_specs=pl.BlockSpec((tq, 3 * PEER_SLOTS), lambda i: (i, 0)),
        out_shape=jax.ShapeDtypeStruct((T, 3 * PEER_SLOTS), F32),
        scratch_shapes=[pltpu.VMEM((PEER_SLOTS, tq), F32)] * 3,
        compiler_params=_params("parallel"),
        name="peer_select",
    )(q, sub)


GATE_GROUP = 16
GATE_HALF = GATE_GROUP // 2
GATE_PITCH = PEER_NKEYS + 8
GATE_SLOTS = 4


def _peer_gates_body(sel_ref, g_ref, *scr_refs, tg):
    n = PEER_SLOTS
    row_id = lax.broadcasted_iota(jnp.int32, (PEER_NKEYS, n), 0).astype(F32)

    def group(gi, scr_ref):
        base = pl.multiple_of(gi * GATE_GROUP, GATE_GROUP)
        for u in range(GATE_GROUP):
            row = sel_ref[pl.ds(base + u, 1), :]
            i1, i2, gate = row[:, 0:n], row[:, n:2 * n], row[:, 2 * n:3 * n]
            p1 = jnp.where(row_id == i1, gate, 0.0).astype(BF16)
            p2 = jnp.where(row_id == i2, 1.0, 0.0).astype(BF16)
            scr_ref[u * GATE_PITCH:u * GATE_PITCH + PEER_NKEYS, :] = lax.dot_general(
                p1, p2, NT_DIMS, preferred_element_type=F32)
        for i1 in range(PEER_NKEYS):
            lo = scr_ref[pl.ds(i1, GATE_HALF, stride=GATE_PITCH), :]
            hi = scr_ref[pl.ds(GATE_HALF * GATE_PITCH + i1, GATE_HALF, stride=GATE_PITCH), :]
            g_ref[pl.ds(base, GATE_GROUP), i1 * PEER_NKEYS:(i1 + 1) * PEER_NKEYS] = (
                jnp.concatenate([lo, hi], axis=0).astype(BF16))

    def groups(gi, carry):
        for slot, scr_ref in enumerate(scr_refs):
            group(gi * GATE_SLOTS + slot, scr_ref)
        return carry

    lax.fori_loop(0, tg // (GATE_GROUP * GATE_SLOTS), groups, 0)


def _peer_gates(sel, tg):
    T = sel.shape[0]
    return pl.pallas_call(
        functools.partial(_peer_gates_body, tg=tg),
        grid=(T // tg,),
        in_specs=[pl.BlockSpec((tg, 3 * PEER_SLOTS), lambda i: (i, 0))],
        out_specs=pl.BlockSpec((tg, PEER_EXPERTS), lambda i: (i, 0)),
        out_shape=jax.ShapeDtypeStruct((T, PEER_EXPERTS), BF16),
        scratch_shapes=[pltpu.VMEM((GATE_GROUP * GATE_PITCH, PEER_NKEYS), F32)] * GATE_SLOTS,
        compiler_params=_params("parallel"),
        name="peer_gates",
    )(sel)


SQRT_HALF = np.float32(np.sqrt(0.5))


def _peer_dense_body(h_ref, g_ref, u_ref, v_ref, x_ref, ng_ref, o_ref, *, close_norm):
    j = pl.program_id(1)

    @pl.when(j == 0)
    def _():
        o_ref[...] = x_ref[...]

    act = lax.dot_general(h_ref[...], u_ref[...], NT_DIMS, preferred_element_type=F32)
    gelu = 0.5 * act * (1.0 + lax.erf(act * SQRT_HALF))
    w = (gelu * g_ref[...].astype(F32)).astype(BF16)
    o_ref[...] += jnp.dot(w, v_ref[...], preferred_element_type=F32)

    if close_norm:
        @pl.when(j == pl.num_programs(1) - 1)
        def _():
            o_ref[...] = _rms(o_ref[...], ng_ref[...])


def _peer_dense(h, gates, u, v, x, norm_g, close_norm, tm, te):
    T, D = x.shape
    E = u.shape[0]
    return pl.pallas_call(
        functools.partial(_peer_dense_body, close_norm=close_norm),
        grid=(T // tm, E // te),
        in_specs=[
            pl.BlockSpec((tm, D), lambda i, j: (i, 0), pipeline_mode=pl.Buffered(1)),
            pl.BlockSpec((tm, te), lambda i, j: (i, j)),
            pl.BlockSpec((te, D), lambda i, j: (j, 0)),
            pl.BlockSpec((te, D), lambda i, j: (j, 0)),
            pl.BlockSpec((tm, D), lambda i, j: (i, 0), pipeline_mode=pl.Buffered(1)),
            pl.BlockSpec((1, D), lambda i, j: (0, 0)),
        ],
        out_specs=pl.BlockSpec((tm, D), lambda i, j: (i, 0), pipeline_mode=pl.Buffered(1)),
        out_shape=jax.ShapeDtypeStruct((T, D), F32),
        compiler_params=_params("parallel", "arbitrary"),
        name="peer_dense",
    )(h, gates, u, v, x, norm_g.reshape(1, D))


def _split_w_in(w):
    kw, vw, rk = GLA_KEY_WIDTH, GLA_VAL_WIDTH, GLA_GATE_RANK
    a0 = 2 * kw + 2 * vw
    main = jnp.concatenate([w[:, :a0], w[:, a0 + 2 * rk:]], axis=1).astype(BF16)
    gate = jnp.pad(w[:, a0:a0 + 2 * rk], ((0, 0), (0, GATE_PAD - 2 * rk))).astype(BF16)
    return main, gate


def _pad_gate_w(wg, first_row):
    return jnp.pad(wg, ((first_row, GATE_PAD - first_row - GLA_GATE_RANK), (0, 0)))


def kernel(x, norm_mix_g, w_in, gla_gate_w_fwd, gla_gate_b_fwd, gla_gate_w_bwd, gla_gate_b_bwd, gla_norm_g, w_up_gla, pool_w, pool_scale, w_up_pool, w_out, norm_ffn_g, peer_wq, peer_subkeys, peer_u, peer_v, norm_final_g):
    B, S, D = x.shape
    T = B * S
    t = _tiles(T, S)
    xf = x.reshape(T, D)
    for l in range(DEPTH):
        w_main, w_gate = _split_w_in(w_in[l])
        proj = _norm_matmul(xf, norm_mix_g[l], w_main, BF16, t["proj_tm"], t["proj_tn"])
        a = _norm_matmul(xf, norm_mix_g[l], w_gate, F32, t["proj_tm"], t["proj_tn"])
        o_bwd = _gla_pass(proj, a, _pad_gate_w(gla_gate_w_bwd[l], GLA_GATE_RANK),
                          gla_gate_b_bwd[l], S, t["gla_tb"], reverse=True)
        y_gla = _gla_pass(proj, a, _pad_gate_w(gla_gate_w_fwd[l], 0), gla_gate_b_fwd[l],
                          S, t["gla_tb"], reverse=False, oprev=o_bwd, norm_g=gla_norm_g[l])
        xf = _mix(y_gla, proj, xf, w_up_gla[l].astype(BF16), pool_w[l].astype(BF16),
                  pool_scale[l], w_up_pool[l].astype(BF16), w_out[l].astype(BF16),
                  S, t["mix_tm"])
        q, h = _norm_matmul(xf, norm_ffn_g[l], peer_wq[l].astype(BF16), F32,
                            t["proj_tm"], t["proj_tn"], emit_h=True)
        sel = _peer_select(q, peer_subkeys[l], t["sel_tq"])
        gates = _peer_gates(sel, t["gate_tg"])
        xf = _peer_dense(h, gates, peer_u[l].astype(BF16), peer_v[l].astype(BF16), xf,
                         norm_final_g, l == DEPTH - 1, t["dense_tm"], t["dense_te"])
    return xf.reshape(B, S, D)
```
